```python
import jax, jax.numpy as jnp
from jax import lax
import numpy as np

D_MODEL = 1024
BATCH = 4
SEQ = 4096
DEPTH = 4

CHUNK = 64
N_MIXERS = 2
FOX_HEADS = 16
FOX_HEAD_DIM = D_MODEL // FOX_HEADS
FOX_Q_BLOCK = 128
FOX_IN = 3 * D_MODEL + FOX_HEADS
GLA_HEADS = 4
GLA_KEY_DIM = D_MODEL // 2
GLA_VAL_DIM = D_MODEL
GLA_DK = GLA_KEY_DIM // GLA_HEADS
GLA_DV = GLA_VAL_DIM // GLA_HEADS
GLA_GATE_RANK = 16
GLA_GATE_TAU = 16.0
GLA_IN = 2 * GLA_KEY_DIM + GLA_VAL_DIM + GLA_GATE_RANK + GLA_VAL_DIM
D_FF = 4 * D_MODEL
EPS = 1e-6
N_FOX_LAYERS = (DEPTH + 1) // 2
N_GLA_LAYERS = DEPTH // 2

kernel_name = 'fox_gla_hybrid_trunk'


def rms_norm(x, g):
    xf = x.astype(jnp.float32)
    y = xf * lax.rsqrt(jnp.mean(xf * xf, axis=-1, keepdims=True) + EPS)
    return (y * g.astype(jnp.float32)).astype(x.dtype)


def fox_mixer(h, w_in, b_f, w_out):
    B, S, _ = h.shape
    H, HD, QB = FOX_HEADS, FOX_HEAD_DIM, FOX_Q_BLOCK
    proj = h @ w_in
    q, k, v, f_logit = jnp.split(proj, [D_MODEL, 2 * D_MODEL, 3 * D_MODEL], axis=-1)
    q = q.reshape(B, S, H, HD)
    k = k.reshape(B, S, H, HD)
    v = v.reshape(B, S, H, HD)
    log_f = jax.nn.log_sigmoid((f_logit + b_f).astype(jnp.float32))
    c = jnp.cumsum(log_f, axis=1).transpose(0, 2, 1)
    nqb = S // QB
    q_blocks = q.reshape(B, nqb, QB, H, HD).transpose(1, 0, 3, 2, 4)
    cq_blocks = c.reshape(B, H, nqb, QB).transpose(2, 0, 1, 3)
    t_blocks = jnp.arange(S).reshape(nqb, QB)
    k_pos = jnp.arange(S)
    scale = HD ** -0.5

    def block(args):
        qb, cqb, tb = args
        logits = jnp.einsum('bhqd,bshd->bhqs', qb, k).astype(jnp.float32) * scale
        logits = logits + cqb[..., :, None] - c[:, :, None, :]
        mask = k_pos[None, :] <= tb[:, None]
        logits = jnp.where(mask, logits, -jnp.inf)
        p = jax.nn.softmax(logits, axis=-1)
        return jnp.einsum('bhqs,bshd->bqhd', p.astype(v.dtype), v)

    o = lax.map(block, (q_blocks, cq_blocks, t_blocks))
    o = o.transpose(1, 0, 2, 3, 4).reshape(B, S, D_MODEL)
    return o @ w_out


def gla_mixer(h, w_in, w_gate_up, b_gate, g_norm, w_out):
    B, S, _ = h.shape
    H, DK, DV, C = GLA_HEADS, GLA_DK, GLA_DV, CHUNK
    n = S // C
    proj = h @ w_in
    splits = [GLA_KEY_DIM, 2 * GLA_KEY_DIM, 2 * GLA_KEY_DIM + GLA_VAL_DIM,
              2 * GLA_KEY_DIM + GLA_VAL_DIM + GLA_GATE_RANK]
    q, k, v, z_lr, r = jnp.split(proj, splits, axis=-1)
    g = jax.nn.log_sigmoid((z_lr @ w_gate_up + b_gate).astype(jnp.float32)) / GLA_GATE_TAU

    def to_chunks(t, d):
        return t.astype(jnp.float32).reshape(B, n, C, H, d).transpose(1, 0, 3, 2, 4)

    qc = to_chunks(q, DK) * (DK ** -0.5)
    kc = to_chunks(k, DK)
    vc = to_chunks(v, DV)
    bc = jnp.cumsum(to_chunks(g, DK), axis=3)
    causal = jnp.tril(jnp.ones((C, C), dtype=bool))

    def step(state, inp):
        qj, kj, vj, bj = inp
        diff = bj[:, :, :, None, :] - bj[:, :, None, :, :]
        decay = jnp.exp(jnp.where(causal[:, :, None], diff, -jnp.inf))
        attn = jnp.einsum('bhtk,bhsk,bhtsk->bhts', qj, kj, decay)
        o = jnp.einsum('bhts,bhsv->bhtv', attn, vj) + \
            jnp.einsum('bhtk,bhkv->bhtv', qj * jnp.exp(bj), state)
        b_last = bj[:, :, -1, :]
        k_dec = kj * jnp.exp(b_last[:, :, None, :] - bj)
        state = state * jnp.exp(b_last)[..., None] + jnp.einsum('bhsk,bhsv->bhkv', k_dec, vj)
        return state, o

    state0 = jnp.zeros((B, H, DK, DV), jnp.float32)
    _, o = lax.scan(step, state0, (qc, kc, vc, bc))
    o = o.transpose(1, 0, 3, 2, 4).reshape(B, S, H, DV)
    o = o * lax.rsqrt(jnp.mean(o * o, axis=-1, keepdims=True) + EPS) * g_norm.astype(jnp.float32)
    o = o * jax.nn.silu(r.astype(jnp.float32)).reshape(B, S, H, DV)
    return o.reshape(B, S, D_MODEL).astype(h.dtype) @ w_out


def sqrelu_mlp(h, w_up, w_down):
    return jnp.square(jax.nn.relu(h @ w_up)) @ w_down


def setup_inputs(seed: int = 0) -> dict:
    key = jax.random.key(seed)
    ks = jax.random.split(key, 14)
    D = D_MODEL
    nrm = jax.random.normal
    x = nrm(ks[0], (BATCH, SEQ, D), jnp.float32)
    fox_w_in = nrm(ks[1], (N_FOX_LAYERS, D, FOX_IN), jnp.float32) * D ** -0.5
    fox_b_f = jax.random.uniform(ks[2], (N_FOX_LAYERS, FOX_HEADS), jnp.float32, 0.0, 3.0)
    fox_w_out = nrm(ks[3], (N_FOX_LAYERS, D, D), jnp.float32) * D ** -0.5
    gla_w_in = nrm(ks[4], (N_GLA_LAYERS, D, GLA_IN), jnp.float32) * D ** -0.5
    gla_w_gate_up = nrm(ks[5], (N_GLA_LAYERS, GLA_GATE_RANK, GLA_KEY_DIM), jnp.float32) * GLA_GATE_RANK ** -0.5
    gla_b_gate = 0.1 * nrm(ks[6], (N_GLA_LAYERS, GLA_KEY_DIM), jnp.float32)
    gla_norm_g = 1.0 + 0.05 * nrm(ks[7], (N_GLA_LAYERS, GLA_DV), jnp.float32)
    gla_w_out = nrm(ks[8], (N_GLA_LAYERS, GLA_VAL_DIM, D), jnp.float32) * GLA_VAL_DIM ** -0.5
    mlp_w_up = nrm(ks[9], (DEPTH, D, D_FF), jnp.float32) * D ** -0.5
    mlp_w_down = nrm(ks[10], (DEPTH, D_FF, D), jnp.float32) * D_FF ** -0.5
    norm_mix_g = 1.0 + 0.05 * nrm(ks[11], (DEPTH, D), jnp.float32)
    norm_mlp_g = 1.0 + 0.05 * nrm(ks[12], (DEPTH, D), jnp.float32)
    norm_final_g = 1.0 + 0.05 * nrm(ks[13], (D,), jnp.float32)
    return {'x': x, 'fox_w_in': fox_w_in, 'fox_b_f': fox_b_f, 'fox_w_out': fox_w_out,
            'gla_w_in': gla_w_in, 'gla_w_gate_up': gla_w_gate_up, 'gla_b_gate': gla_b_gate,
            'gla_norm_g': gla_norm_g, 'gla_w_out': gla_w_out,
            'mlp_w_up': mlp_w_up, 'mlp_w_down': mlp_w_down,
            'norm_mix_g': norm_mix_g, 'norm_mlp_g': norm_mlp_g, 'norm_final_g': norm_final_g}


def reference(x, fox_w_in, fox_b_f, fox_w_out, gla_w_in, gla_w_gate_up, gla_b_gate,
              gla_norm_g, gla_w_out, mlp_w_up, mlp_w_down, norm_mix_g, norm_mlp_g,
              norm_final_g):
    h = x
    for i in range(DEPTH):
        j = i // N_MIXERS
        hn = rms_norm(h, norm_mix_g[i])
        if i % N_MIXERS == 0:
            h = h + fox_mixer(hn, fox_w_in[j], fox_b_f[j], fox_w_out[j])
        else:
            h = h + gla_mixer(hn, gla_w_in[j], gla_w_gate_up[j], gla_b_gate[j],
                              gla_norm_g[j], gla_w_out[j])
        h = h + sqrelu_mlp(rms_norm(h, norm_mlp_g[i]), mlp_w_up[i], mlp_w_down[i])
    return rms_norm(h, norm_final_g)
```

```python
import functools

import numpy as np
import jax
import jax.numpy as jnp
from jax import lax
from jax.experimental import pallas as pl
from jax.experimental.pallas import tpu as pltpu

F32 = jnp.float32
BF16 = jnp.bfloat16

D_MODEL = 1024
BATCH = 4
SEQ = 4096
TOKENS = BATCH * SEQ
DEPTH = 4
EPS = 1e-6

FOX_HEADS = 16
FOX_HEAD_DIM = D_MODEL // FOX_HEADS
FOX_SCALE = FOX_HEAD_DIM ** -0.5
FOX_GATE_LANES = 8
GLA_HEADS = 4
GLA_KEY_DIM = D_MODEL // 2
GLA_DK = GLA_KEY_DIM // GLA_HEADS
GLA_DV = D_MODEL // GLA_HEADS
GLA_GATE_RANK = 16
GLA_GATE_TAU = 16.0
GLA_SCALE = GLA_DK ** -0.5
CHUNK = 64
D_FF = 4 * D_MODEL

LANES = 128
VMEM_LIMIT = 48 * 1024 * 1024

PROJ_TM = 512
MLP_TM = 1024
MLP_TF = 512
ATT_TQ = 256
ATT_TK = 256
GATE_BLOCK = 256

NT_DIMS = (((1,), (1,)), ((), ()))
TN_DIMS = (((0,), (0,)), ((), ()))


def _params(*semantics):
    return pltpu.CompilerParams(dimension_semantics=semantics,
                                vmem_limit_bytes=VMEM_LIMIT)


def _rms(x, g):
    ms = jnp.mean(x * x, axis=-1, keepdims=True)
    return x * lax.rsqrt(ms + EPS) * g


def _log_sigmoid(x):
    return jnp.minimum(x, 0.0) - jnp.log1p(jnp.exp(-jnp.abs(x)))


def _split3(x):
    hi = x.astype(BF16)
    r1 = x - hi.astype(F32)
    mid = r1.astype(BF16)
    lo = (r1 - mid.astype(F32)).astype(BF16)
    return hi, mid, lo


def _fox_inproj_kernel(x_ref, g_ref, wq_ref, wk_ref, wvt_ref, wf_ref,
                       q_ref, k_ref, vt_ref, f_ref):
    hn = _rms(x_ref[...], g_ref[...]).astype(BF16)
    q = jnp.dot(hn, wq_ref[...], preferred_element_type=F32)
    q_ref[...] = (q * FOX_SCALE).astype(BF16)
    k_ref[...] = jnp.dot(hn, wk_ref[...], preferred_element_type=F32).astype(BF16)
    vt = lax.dot_general(wvt_ref[...], hn, NT_DIMS, preferred_element_type=F32)
    vt_ref[0] = vt.astype(BF16)
    f_ref[...] = jnp.dot(hn, wf_ref[...], preferred_element_type=F32)


def _fox_inproj(h, g, wq, wk, wvt, wf):
    tm = PROJ_TM
    per_b = SEQ // tm
    full = lambda i: (0, 0)
    return pl.pallas_call(
        _fox_inproj_kernel,
        grid=(TOKENS // tm,),
        in_specs=[
            pl.BlockSpec((tm, D_MODEL), lambda i: (i, 0)),
            pl.BlockSpec((1, D_MODEL), full),
            pl.BlockSpec((D_MODEL, D_MODEL), full),
            pl.BlockSpec((D_MODEL, D_MODEL), full),
            pl.BlockSpec((D_MODEL, D_MODEL), full),
            pl.BlockSpec((D_MODEL, LANES), full),
        ],
        out_specs=[
            pl.BlockSpec((tm, D_MODEL), lambda i: (i, 0)),
            pl.BlockSpec((tm, D_MODEL), lambda i: (i, 0)),
            pl.BlockSpec((1, D_MODEL, tm), lambda i: (i // per_b, 0, i % per_b)),
            pl.BlockSpec((tm, LANES), lambda i: (i, 0)),
        ],
        out_shape=[
            jax.ShapeDtypeStruct((TOKENS, D_MODEL), BF16),
            jax.ShapeDtypeStruct((TOKENS, D_MODEL), BF16),
            jax.ShapeDtypeStruct((BATCH, D_MODEL, SEQ), BF16),
            jax.ShapeDtypeStruct((TOKENS, LANES), F32),
        ],
        compiler_params=_params("parallel"),
        name="fox_inproj",
    )(h, g, wq, wk, wvt, wf)


def _fox_gate_kernel(f_ref, bf_ref, cq_ref, ck_ref):
    bl = GATE_BLOCK
    row = lax.broadcasted_iota(jnp.int32, (bl, bl), 0)
    col = lax.broadcasted_iota(jnp.int32, (bl, bl), 1)
    tril = (col <= row).astype(BF16)
    j = lax.broadcasted_iota(jnp.int32, (1, LANES), 1) % FOX_GATE_LANES

    def body(i, carry):
        r0 = pl.multiple_of(i * bl, bl)
        lf = _log_sigmoid(f_ref[pl.ds(r0, bl), :] + bf_ref[...])
        hi, mid, lo = _split3(lf)
        c = carry + (jnp.dot(tril, hi, preferred_element_type=F32)
                     + jnp.dot(tril, mid, preferred_element_type=F32)
                     + jnp.dot(tril, lo, preferred_element_type=F32))
        chi, cmid, clo = (t.astype(F32) for t in _split3(c))
        cq = jnp.where(j == 0, chi, jnp.where(j == 1, cmid, jnp.where(
            j == 2, clo, jnp.where(j < 6, 1.0, 0.0))))
        ck = jnp.where(j < 3, 1.0, jnp.where(j == 3, -chi, jnp.where(
            j == 4, -cmid, jnp.where(j == 5, -clo, 0.0))))
        cq_ref[pl.ds(r0, bl), :] = cq.astype(BF16)
        ck_ref[pl.ds(r0, bl), :] = ck.astype(BF16)
        return c[bl - 1:bl, :]

    lax.fori_loop(0, SEQ // bl, body, jnp.zeros((1, LANES), F32))


def _fox_gate(f, bf):
    return pl.pallas_call(
        _fox_gate_kernel,
        grid=(BATCH,),
        in_specs=[pl.BlockSpec((SEQ, LANES), lambda b: (b, 0)),
                  pl.BlockSpec((1, LANES), lambda b: (0, 0))],
        out_specs=[pl.BlockSpec((SEQ, LANES), lambda b: (b, 0)),
                   pl.BlockSpec((SEQ, LANES), lambda b: (b, 0))],
        out_shape=[jax.ShapeDtypeStruct((TOKENS, LANES), BF16),
                   jax.ShapeDtypeStruct((TOKENS, LANES), BF16)],
        compiler_params=_params("parallel"),
        name="fox_gate",
    )(f, bf)


def _fox_attn_kernel(q_ref, k_ref, cq_ref, ck_ref, vt_ref, o_ref):
    pair = pl.program_id(1)
    tq, tk, hd = ATT_TQ, ATT_TK, FOX_HEAD_DIM
    lane = lax.broadcasted_iota(jnp.int32, (1, LANES), 1)
    s_idx = lax.broadcasted_iota(jnp.int32, (tk, tq), 0)
    t_idx = lax.broadcasted_iota(jnp.int32, (tk, tq), 1)
    causal = s_idx <= t_idx

    def q_body(qi, _):
        q0 = pl.multiple_of(qi * tq, tq)
        q_t = q_ref[pl.ds(q0, tq), :].astype(F32)
        cq_t = cq_ref[pl.ds(q0, tq), :].astype(F32)
        outs = []
        for hh in range(2):
            head = 2 * pair + hh
            qcat = jnp.concatenate(
                [jnp.where(lane // hd == hh, q_t, 0.0).astype(BF16),
                 jnp.where(lane // FOX_GATE_LANES == head, cq_t, 0.0).astype(BF16)],
                axis=1)

            def tile(kj, carry, masked, qcat=qcat, hh=hh):
                m, l, acc = carry
                k0 = pl.multiple_of(kj * tk, tk)
                kcat = jnp.concatenate(
                    [k_ref[pl.ds(k0, tk), :], ck_ref[pl.ds(k0, tk), :]], axis=1)
                s = lax.dot_general(kcat, qcat, NT_DIMS,
                                    preferred_element_type=F32)
                if masked:
                    s = jnp.where(causal, s, -jnp.inf)
                m_new = jnp.maximum(m, jnp.max(s, axis=0, keepdims=True))
                alpha = jnp.exp(m - m_new)
                p = jnp.exp(s - m_new)
                l = alpha * l + jnp.sum(p, axis=0, keepdims=True)
                vt = vt_ref[0, pl.ds(hh * hd, hd), pl.ds(k0, tk)]
                acc = alpha * acc + jnp.dot(vt, p.astype(BF16),
                                            preferred_element_type=F32)
                return m_new, l, acc

            init = (jnp.full((1, tq), -jnp.inf, F32), jnp.zeros((1, tq), F32),
                    jnp.zeros((hd, tq), F32))
            carry = lax.fori_loop(
                0, qi, functools.partial(tile, masked=False), init)
            _, l, acc = tile(qi, carry, True)
            outs.append(acc / l)
        o = jnp.concatenate(outs, axis=0)
        o_ref[pl.ds(q0, tq), :] = o.T.astype(BF16)
        return 0

    lax.fori_loop(0, SEQ // tq, q_body, 0)


def _fox_attn(q, k, cq, ck, vt):
    pairs = FOX_HEADS // 2
    return pl.pallas_call(
        _fox_attn_kernel,
        grid=(BATCH, pairs),
        in_specs=[
            pl.BlockSpec((SEQ, LANES), lambda b, p: (b, p)),
            pl.BlockSpec((SEQ, LANES), lambda b, p: (b, p)),
            pl.BlockSpec((SEQ, LANES), lambda b, p: (b, 0)),
            pl.BlockSpec((SEQ, LANES), lambda b, p: (b, 0)),
            pl.BlockSpec((1, LANES, SEQ), lambda b, p: (b, p, 0)),
        ],
        out_specs=pl.BlockSpec((SEQ, LANES), lambda b, p: (b, p)),
        out_shape=jax.ShapeDtypeStruct((TOKENS, D_MODEL), BF16),
        compiler_params=_params("parallel", "parallel"),
        name="fox_attn",
    )(q, k, cq, ck, vt)


def _outproj_kernel(a_ref, w_ref, res_ref, o_ref):
    o_ref[...] = res_ref[...] + jnp.dot(a_ref[...], w_ref[...],
                                        preferred_element_type=F32)


def _outproj(a, w, res):
    tm = PROJ_TM
    return pl.pallas_call(
        _outproj_kernel,
        grid=(TOKENS // tm,),
        in_specs=[pl.BlockSpec((tm, D_MODEL), lambda i: (i, 0)),
                  pl.BlockSpec((D_MODEL, D_MODEL), lambda i: (0, 0)),
                  pl.BlockSpec((tm, D_MODEL), lambda i: (i, 0))],
        out_specs=pl.BlockSpec((tm, D_MODEL), lambda i: (i, 0)),
        out_shape=jax.ShapeDtypeStruct((TOKENS, D_MODEL), F32),
        compiler_params=_params("parallel"),
        name="outproj",
    )(a, w, res)


def _gla_inproj_kernel(x_ref, g_ref, wq_ref, wk_ref, wv_ref, wr_ref, wz_ref,
                       wg_ref, bg_ref, q_ref, k_ref, v_ref, r_ref, gate_ref):
    hn = _rms(x_ref[...], g_ref[...]).astype(BF16)
    q = jnp.dot(hn, wq_ref[...], preferred_element_type=F32)
    q_ref[...] = q * GLA_SCALE
    k_ref[...] = jnp.dot(hn, wk_ref[...], preferred_element_type=F32)
    v_ref[...] = jnp.dot(hn, wv_ref[...], preferred_element_type=F32).astype(BF16)
    r_ref[...] = jnp.dot(hn, wr_ref[...], preferred_element_type=F32)
    z = jnp.dot(hn, wz_ref[...], preferred_element_type=F32).astype(BF16)
    zg = jnp.dot(z, wg_ref[...], preferred_element_type=F32) + bg_ref[...]
    gate_ref[...] = _log_sigmoid(zg) / GLA_GATE_TAU


def _gla_inproj(h, g, wq, wk, wv, wr, wz, wg, bg):
    tm = PROJ_TM
    full = lambda i: (0, 0)
    rows = lambda i: (i, 0)
    return pl.pallas_call(
        _gla_inproj_kernel,
        grid=(TOKENS // tm,),
        in_specs=[
            pl.BlockSpec((tm, D_MODEL), rows),
            pl.BlockSpec((1, D_MODEL), full),
            pl.BlockSpec((D_MODEL, GLA_KEY_DIM), full),
            pl.BlockSpec((D_MODEL, GLA_KEY_DIM), full),
            pl.BlockSpec((D_MODEL, D_MODEL), full),
            pl.BlockSpec((D_MODEL, D_MODEL), full),
            pl.BlockSpec((D_MODEL, LANES), full),
            pl.BlockSpec((LANES, GLA_KEY_DIM), full),
            pl.BlockSpec((1, GLA_KEY_DIM), full),
        ],
        out_specs=[
            pl.BlockSpec((tm, GLA_KEY_DIM), rows),
            pl.BlockSpec((tm, GLA_KEY_DIM), rows),
            pl.BlockSpec((tm, D_MODEL), rows),
            pl.BlockSpec((tm, D_MODEL), rows),
            pl.BlockSpec((tm, GLA_KEY_DIM), rows),
        ],
        out_shape=[
            jax.ShapeDtypeStruct((TOKENS, GLA_KEY_DIM), F32),
            jax.ShapeDtypeStruct((TOKENS, GLA_KEY_DIM), F32),
            jax.ShapeDtypeStruct((TOKENS, D_MODEL), BF16),
            jax.ShapeDtypeStruct((TOKENS, D_MODEL), F32),
            jax.ShapeDtypeStruct((TOKENS, GLA_KEY_DIM), F32),
        ],
        compiler_params=_params("parallel"),
        name="gla_inproj",
    )(h, g, wq, wk, wv, wr, wz, wg, bg)


GLA_LEVELS = (1, 2, 4, 8, 16, 32)


def _gla_selection_matrix():
    c = CHUNK
    t = np.arange(c)[:, None]
    r = np.arange(c)[None, :]
    mats = []
    for m in GLA_LEVELS[1:]:
        mats.append((r > m * (t // m)) & (r <= t))
    mats.append(r <= t)
    for m in GLA_LEVELS + (c,):
        nxt = np.minimum(m * (t // m + 1), c - 1)
        mats.append((r > t) & (r <= nxt))
    return np.concatenate(mats, axis=0).astype(np.float32)


GLA_SEL_ROWS = (len(GLA_LEVELS) - 1 + 1 + len(GLA_LEVELS) + 1) * CHUNK


def _gla_chunk_kernel(q_ref, k_ref, v_ref, r_ref, gate_ref, sel_ref, gn_ref,
                      y_ref, state_ref):
    c, nh, dk, dv = CHUNK, GLA_HEADS, GLA_DK, GLA_DV

    @pl.when(pl.program_id(1) == 0)
    def _():
        state_ref[...] = jnp.zeros_like(state_ref)

    q = q_ref[...]
    k = k_ref[...]
    v = v_ref[...]
    gate = gate_ref[...]
    g_hi = gate.astype(BF16)
    g_lo = (gate - g_hi.astype(F32)).astype(BF16)
    sel = sel_ref[...]
    decay = jnp.exp(jnp.dot(sel, g_hi, preferred_element_type=F32)
                    + jnp.dot(sel, g_lo, preferred_element_type=F32))

    def dq(i):
        return decay[i * c:(i + 1) * c, :]

    n_q = len(GLA_LEVELS)
    def dk_(i):
        return decay[(n_q + i) * c:(n_q + i + 1) * c, :]

    def stack_heads(x):
        return jnp.concatenate([x[:, h * dk:(h + 1) * dk] for h in range(nh)], axis=0)

    row = lax.broadcasted_iota(jnp.int32, (nh * c, nh * c), 0)
    col = lax.broadcasted_iota(jnp.int32, (nh * c, nh * c), 1)
    x = row ^ col
    lower = row > col

    def scores(qq, kk):
        return lax.dot_general(stack_heads(qq.astype(BF16)), stack_heads(kk.astype(BF16)),
                               NT_DIMS, preferred_element_type=F32)

    attn = jnp.where(x == 0, scores(q, k), 0.0)
    for li, m in enumerate(GLA_LEVELS):
        qq = q if m == 1 else q * dq(li - 1)
        kk = k * dk_(li)
        attn = attn + jnp.where((x >= m) & (x < 2 * m) & lower, scores(qq, kk), 0.0)
    attn = attn.astype(BF16)

    qs = (q * dq(n_q - 1)).astype(BF16)
    kd = (k * dk_(len(GLA_LEVELS))).astype(BF16)
    ones = jnp.ones((c, dv), BF16)
    gn = gn_ref[...]
    for h in range(nh):
        v_h = v[:, h * dv:(h + 1) * dv]
        state = state_ref[h]
        o = jnp.dot(attn[h * c:(h + 1) * c, h * c:(h + 1) * c], v_h,
                    preferred_element_type=F32)
        o = o + jnp.dot(qs[:, h * dk:(h + 1) * dk], state.astype(BF16),
                        preferred_element_type=F32)
        b_last = (lax.dot_general(g_hi[:, h * dk:(h + 1) * dk], ones, TN_DIMS,
                                  preferred_element_type=F32)
                  + lax.dot_general(g_lo[:, h * dk:(h + 1) * dk], ones, TN_DIMS,
                                    preferred_element_type=F32))
        state_ref[h] = state * jnp.exp(b_last) + lax.dot_general(
            kd[:, h * dk:(h + 1) * dk], v_h, TN_DIMS, preferred_element_type=F32)
        o = _rms(o, gn)
        r = r_ref[:, h * dv:(h + 1) * dv]
        y_ref[:, h * dv:(h + 1) * dv] = (o * (r * jax.nn.sigmoid(r))).astype(BF16)


def _gla_chunks(q, k, v, r, gate, sel, gn):
    n = SEQ // CHUNK
    rows = lambda b, j: (b * n + j, 0)
    full = lambda b, j: (0, 0)
    return pl.pallas_call(
        _gla_chunk_kernel,
        grid=(BATCH, n),
        in_specs=[
            pl.BlockSpec((CHUNK, GLA_KEY_DIM), rows),
            pl.BlockSpec((CHUNK, GLA_KEY_DIM), rows),
            pl.BlockSpec((CHUNK, D_MODEL), rows),
            pl.BlockSpec((CHUNK, D_MODEL), rows),
            pl.BlockSpec((CHUNK, GLA_KEY_DIM), rows),
            pl.BlockSpec((GLA_SEL_ROWS, CHUNK), full),
            pl.BlockSpec((1, GLA_DV), full),
        ],
        out_specs=pl.BlockSpec((CHUNK, D_MODEL), rows),
        out_shape=jax.ShapeDtypeStruct((TOKENS, D_MODEL), BF16),
        scratch_shapes=[pltpu.VMEM((GLA_HEADS, GLA_DK, GLA_DV), F32)],
        compiler_params=_params("parallel", "arbitrary"),
        name="gla_chunks",
    )(q, k, v, r, gate, sel, gn)


def _mlp_kernel(h_ref, g_ref, wup_ref, wdn_ref, gf_ref, o_ref, hn_ref, *, final_norm):
    kf = pl.program_id(1)

    @pl.when(kf == 0)
    def _():
        x = h_ref[...]
        hn_ref[...] = _rms(x, g_ref[...]).astype(BF16)
        o_ref[...] = x

    u = jnp.dot(hn_ref[...], wup_ref[...], preferred_element_type=F32)
    u = jnp.maximum(u, 0.0)
    u = (u * u).astype(BF16)
    o_ref[...] += jnp.dot(u, wdn_ref[...], preferred_element_type=F32)

    if final_norm:
        @pl.when(kf == pl.num_programs(1) - 1)
        def _():
            o_ref[...] = _rms(o_ref[...], gf_ref[...])


def _mlp(h, g, wup, wdn, gf, final_norm):
    tm, tf = MLP_TM, MLP_TF
    return pl.pallas_call(
        functools.partial(_mlp_kernel, final_norm=final_norm),
        grid=(TOKENS // tm, D_FF // tf),
        in_specs=[
            pl.BlockSpec((tm, D_MODEL), lambda i, f: (i, 0)),
            pl.BlockSpec((1, D_MODEL), lambda i, f: (0, 0)),
            pl.BlockSpec((D_MODEL, tf), lambda i, f: (0, f)),
            pl.BlockSpec((tf, D_MODEL), lambda i, f: (f, 0)),
            pl.BlockSpec((1, D_MODEL), lambda i, f: (0, 0)),
        ],
        out_specs=pl.BlockSpec((tm, D_MODEL), lambda i, f: (i, 0)),
        out_shape=jax.ShapeDtypeStruct((TOKENS, D_MODEL), F32),
        scratch_shapes=[pltpu.VMEM((tm, D_MODEL), BF16)],
        compiler_params=_params("parallel", "arbitrary"),
        name="mlp",
    )(h, g, wup, wdn, gf)


def _row(v):
    return v.reshape(1, -1).astype(F32)


def _fox_layer(h, g, w_in, b_f, w_out):
    d = D_MODEL
    wq = w_in[:, :d].astype(BF16)
    wk = w_in[:, d:2 * d].astype(BF16)
    wvt = w_in[:, 2 * d:3 * d].T.astype(BF16)
    wf = jnp.repeat(w_in[:, 3 * d:], FOX_GATE_LANES, axis=1).astype(BF16)
    bf = _row(jnp.repeat(b_f, FOX_GATE_LANES))
    q, k, vt, f = _fox_inproj(h, _row(g), wq, wk, wvt, wf)
    cq, ck = _fox_gate(f, bf)
    o = _fox_attn(q, k, cq, ck, vt)
    return _outproj(o, w_out.astype(BF16), h)


def _gla_layer(h, g, w_in, w_gate_up, b_gate, g_norm, w_out, sel):
    kd, d = GLA_KEY_DIM, D_MODEL
    wq = w_in[:, :kd].astype(BF16)
    wk = w_in[:, kd:2 * kd].astype(BF16)
    wv = w_in[:, 2 * kd:2 * kd + d].astype(BF16)
    z0 = 2 * kd + d
    wz = jnp.pad(w_in[:, z0:z0 + GLA_GATE_RANK],
                 ((0, 0), (0, LANES - GLA_GATE_RANK))).astype(BF16)
    wr = w_in[:, z0 + GLA_GATE_RANK:].astype(BF16)
    wg = jnp.pad(w_gate_up, ((0, LANES - GLA_GATE_RANK), (0, 0))).astype(BF16)
    q, k, v, r, gate = _gla_inproj(h, _row(g), wq, wk, wv, wr, wz, wg, _row(b_gate))
    y = _gla_chunks(q, k, v, r, gate, sel, _row(g_norm))
    return _outproj(y, w_out.astype(BF16), h)


def kernel(x, fox_w_in, fox_b_f, fox_w_out, gla_w_in, gla_w_gate_up, gla_b_gate,
           gla_norm_g, gla_w_out, mlp_w_up, mlp_w_down, norm_mix_g, norm_mlp_g,
           norm_final_g):
    assert x.shape == (BATCH, SEQ, D_MODEL) and x.dtype == F32
    sel = jnp.asarray(_gla_selection_matrix(), BF16)
    h = x.reshape(TOKENS, D_MODEL)
    for i in range(DEPTH):
        j = i // 2
        if i % 2 == 0:
            h = _fox_layer(h, norm_mix_g[i], fox_w_in[j], fox_b_f[j], fox_w_out[j])
        else:
            h = _gla_layer(h, norm_mix_g[i], gla_w_in[j], gla_w_gate_up[j],
                           gla_b_gate[j], gla_norm_g[j], gla_w_out[j], sel)
        h = _mlp(h, _row(norm_mlp_g[i]), mlp_w_up[i].astype(BF16),
                 mlp_w_down[i].astype(BF16), _row(norm_final_g),
                 final_norm=(i == DEPTH - 1))
    return h.reshape(BATCH, SEQ, D_MODEL)
```

```python
import functools

import numpy as np
import jax
import jax.numpy as jnp
from jax import lax
from jax.experimental import pallas as pl
from jax.experimental.pallas import tpu as pltpu

F32 = jnp.float32
BF16 = jnp.bfloat16

D_MODEL = 1024
BATCH = 4
SEQ = 4096
TOKENS = BATCH * SEQ
DEPTH = 4
EPS = 1e-6

FOX_HEADS = 16
FOX_HEAD_DIM = D_MODEL // FOX_HEADS
FOX_SCALE = FOX_HEAD_DIM ** -0.5
FOX_GATE_LANES = 8
GLA_HEADS = 4
GLA_KEY_DIM = D_MODEL // 2
GLA_DK = GLA_KEY_DIM // GLA_HEADS
GLA_DV = D_MODEL // GLA_HEADS
GLA_GATE_RANK = 16
GLA_GATE_TAU = 16.0
GLA_SCALE = GLA_DK ** -0.5
CHUNK = 64
D_FF = 4 * D_MODEL

LANES = 128
VMEM_LIMIT = 48 * 1024 * 1024

PROJ_TM = 512
MLP_TM = 1024
MLP_TF = 512
ATT_TQ = 1024
ATT_TK = 512
assert ATT_TQ % ATT_TK == 0
ATT_STRIP = 512
ATT_ONES_ROWS = 16
LOG2E = 1.4426950408889634
GATE_BLOCK = 256

NT_DIMS = (((1,), (1,)), ((), ()))
TN_DIMS = (((0,), (0,)), ((), ()))


def _params(*semantics):
    return pltpu.CompilerParams(dimension_semantics=semantics,
                                vmem_limit_bytes=VMEM_LIMIT)


def _rms(x, g):
    ms = jnp.mean(x * x, axis=-1, keepdims=True)
    return x * lax.rsqrt(ms + EPS) * g


def _log_sigmoid(x):
    return jnp.minimum(x, 0.0) - jnp.log1p(jnp.exp(-jnp.abs(x)))


def _split3(x):
    hi = x.astype(BF16)
    r1 = x - hi.astype(F32)
    mid = r1.astype(BF16)
    lo = (r1 - mid.astype(F32)).astype(BF16)
    return hi, mid, lo


def _fox_inproj_kernel(x_ref, g_ref, wq_ref, wk_ref, wvt_ref, wf_ref,
                       q_ref, k_ref, vt_ref, f_ref):
    hn = _rms(x_ref[...], g_ref[...]).astype(BF16)
    q = jnp.dot(hn, wq_ref[...], preferred_element_type=F32)
    q_ref[...] = (q * (FOX_SCALE * LOG2E)).astype(BF16)
    k_ref[...] = jnp.dot(hn, wk_ref[...], preferred_element_type=F32).astype(BF16)
    vt = lax.dot_general(wvt_ref[...], hn, NT_DIMS, preferred_element_type=F32)
    vt_ref[0] = vt.astype(BF16)
    f_ref[...] = jnp.dot(hn, wf_ref[...], preferred_element_type=F32)


def _fox_inproj(h, g, wq, wk, wvt, wf):
    tm = PROJ_TM
    per_b = SEQ // tm
    full = lambda i: (0, 0)
    return pl.pallas_call(
        _fox_inproj_kernel,
        grid=(TOKENS // tm,),
        in_specs=[
            pl.BlockSpec((tm, D_MODEL), lambda i: (i, 0)),
            pl.BlockSpec((1, D_MODEL), full),
            pl.BlockSpec((D_MODEL, D_MODEL), full),
            pl.BlockSpec((D_MODEL, D_MODEL), full),
            pl.BlockSpec((D_MODEL, D_MODEL), full),
            pl.BlockSpec((D_MODEL, LANES), full),
        ],
        out_specs=[
            pl.BlockSpec((tm, D_MODEL), lambda i: (i, 0)),
            pl.BlockSpec((tm, D_MODEL), lambda i: (i, 0)),
            pl.BlockSpec((1, D_MODEL, tm), lambda i: (i // per_b, 0, i % per_b)),
            pl.BlockSpec((tm, LANES), lambda i: (i, 0)),
        ],
        out_shape=[
            jax.ShapeDtypeStruct((TOKENS, D_MODEL), BF16),
            jax.ShapeDtypeStruct((TOKENS, D_MODEL), BF16),
            jax.ShapeDtypeStruct((BATCH, D_MODEL, SEQ), BF16),
            jax.ShapeDtypeStruct((TOKENS, LANES), F32),
        ],
        compiler_params=_params("parallel"),
        name="fox_inproj",
    )(h, g, wq, wk, wvt, wf)


def _fox_gate_kernel(f_ref, bf_ref, cq_ref, ck_ref):
    bl = GATE_BLOCK
    row = lax.broadcasted_iota(jnp.int32, (bl, bl), 0)
    col = lax.broadcasted_iota(jnp.int32, (bl, bl), 1)
    tril = (col <= row).astype(BF16)
    j = lax.broadcasted_iota(jnp.int32, (1, LANES), 1) % FOX_GATE_LANES

    def body(i, carry):
        r0 = pl.multiple_of(i * bl, bl)
        lf = _log_sigmoid(f_ref[pl.ds(r0, bl), :] + bf_ref[...])
        hi, mid, lo = _split3(lf)
        c = carry + (jnp.dot(tril, hi, preferred_element_type=F32)
                     + jnp.dot(tril, mid, preferred_element_type=F32)
                     + jnp.dot(tril, lo, preferred_element_type=F32))
        chi, cmid, clo = (t.astype(F32) for t in _split3(c * LOG2E))
        cq = jnp.where(j == 0, chi, jnp.where(j == 1, cmid, jnp.where(
            j == 2, clo, jnp.where(j < 6, 1.0, 0.0))))
        ck = jnp.where(j < 3, 1.0, jnp.where(j == 3, -chi, jnp.where(
            j == 4, -cmid, jnp.where(j == 5, -clo, 0.0))))
        cq_ref[pl.ds(r0, bl), :] = cq.astype(BF16)
        ck_ref[pl.ds(r0, bl), :] = ck.astype(BF16)
        return c[bl - 1:bl, :]

    lax.fori_loop(0, SEQ // bl, body, jnp.zeros((1, LANES), F32))


def _fox_gate(f, bf):
    return pl.pallas_call(
        _fox_gate_kernel,
        grid=(BATCH,),
        in_specs=[pl.BlockSpec((SEQ, LANES), lambda b: (b, 0)),
                  pl.BlockSpec((1, LANES), lambda b: (0, 0))],
        out_specs=[pl.BlockSpec((SEQ, LANES), lambda b: (b, 0)),
                   pl.BlockSpec((SEQ, LANES), lambda b: (b, 0))],
        out_shape=[jax.ShapeDtypeStruct((TOKENS, LANES), BF16),
                   jax.ShapeDtypeStruct((TOKENS, LANES), BF16)],
        compiler_params=_params("parallel"),
        name="fox_gate",
    )(f, bf)


def _fox_attn_kernel(q_ref, k_ref, cq_ref, ck_ref, vt_ref, o_ref):
    pair = pl.program_id(1)
    tq, tk, hd, w = ATT_TQ, ATT_TK, FOX_HEAD_DIM, ATT_STRIP
    lane = lax.broadcasted_iota(jnp.int32, (1, LANES), 1)

    def q_body(qi, _):
        q0 = pl.multiple_of(qi * tq, tq)
        q_t = q_ref[pl.ds(q0, tq), :].astype(F32)
        cq_t = cq_ref[pl.ds(q0, tq), :].astype(F32)
        qcats = [
            jnp.concatenate(
                [jnp.where(lane // hd == hh, q_t, 0.0).astype(BF16),
                 jnp.where(lane // FOX_GATE_LANES == 2 * pair + hh, cq_t,
                           0.0).astype(BF16)], axis=1)
            for hh in range(2)]

        units = [(hh, st) for st in range(tq // w) for hh in range(2)]
        ones = jnp.ones((ATT_ONES_ROWS, tk), BF16)

        def tile(kj, carry, diag=None):
            k0 = pl.multiple_of(kj * tk, tk)
            kcat = jnp.concatenate(
                [k_ref[pl.ds(k0, tk), :], ck_ref[pl.ds(k0, tk), :]], axis=1)
            live = [u for u in range(len(units))
                    if diag is None or (units[u][1] + 1) * w > diag * tk]

            def scores(u):
                hh, st = units[u]
                s = lax.dot_general(kcat, qcats[hh][st * w:(st + 1) * w, :], NT_DIMS,
                                    preferred_element_type=F32)
                if diag is not None and (diag + 1) * tk > st * w + 1:
                    s_idx = lax.broadcasted_iota(jnp.int32, (tk, w), 0) + diag * tk
                    t_idx = lax.broadcasted_iota(jnp.int32, (tk, w), 1) + st * w
                    s = jnp.where(s_idx <= t_idx, s, -jnp.inf)
                return s

            def softmax(u, s):
                m = carry[u][0]
                m_new = jnp.maximum(m, jnp.max(s, axis=0, keepdims=True))
                return m_new, jnp.exp2(m - m_new), jnp.exp2(s - m_new).astype(BF16)

            def weighted(u, alpha, p):
                hh, _ = units[u]
                vt = jnp.concatenate(
                    [vt_ref[0, pl.ds(hh * hd, hd), pl.ds(k0, tk)], ones], axis=0)
                return alpha * carry[u][1] + jnp.dot(vt, p, preferred_element_type=F32)

            n = len(live)
            s_of, sm_of, out = {}, {}, list(carry)
            for step in range(n + 2):
                if step < n:
                    s_of[step] = scores(live[step])
                if 1 <= step <= n:
                    sm_of[step - 1] = softmax(live[step - 1], s_of.pop(step - 1))
                if step >= 2:
                    m_new, alpha, p = sm_of.pop(step - 2)
                    out[live[step - 2]] = (m_new, weighted(live[step - 2], alpha, p))
            return tuple(out)

        init = tuple((jnp.full((1, w), -jnp.inf, F32),
                      jnp.zeros((hd + ATT_ONES_ROWS, w), F32)) for _ in units)
        n_full = qi * (tq // tk)
        carry = lax.fori_loop(0, n_full, tile, init)
        for d in range(tq // tk):
            carry = tile(n_full + d, carry, diag=d)
        heads = []
        for hh in range(2):
            acc = jnp.concatenate([carry[u][1] for u in range(len(units))
                                   if units[u][0] == hh], axis=1)
            heads.append(acc[:hd] / acc[hd:hd + 1])
        o = jnp.concatenate(heads, axis=0)
        o_ref[pl.ds(q0, tq), :] = o.T.astype(BF16)
        return 0

    lax.fori_loop(0, SEQ // tq, q_body, 0)


def _fox_attn(q, k, cq, ck, vt):
    pairs = FOX_HEADS // 2
    return pl.pallas_call(
        _fox_attn_kernel,
        grid=(BATCH, pairs),
        in_specs=[
            pl.BlockSpec((SEQ, LANES), lambda b, p: (b, p)),
            pl.BlockSpec((SEQ, LANES), lambda b, p: (b, p)),
            pl.BlockSpec((SEQ, LANES), lambda b, p: (b, 0)),
            pl.BlockSpec((SEQ, LANES), lambda b, p: (b, 0)),
            pl.BlockSpec((1, LANES, SEQ), lambda b, p: (b, p, 0)),
        ],
        out_specs=pl.BlockSpec((SEQ, LANES), lambda b, p: (b, p)),
        out_shape=jax.ShapeDtypeStruct((TOKENS, D_MODEL), BF16),
        compiler_params=_params("parallel", "parallel"),
        name="fox_attn",
    )(q, k, cq, ck, vt)


def _outproj_kernel(a_ref, w_ref, res_ref, o_ref):
    o_ref[...] = res_ref[...] + jnp.dot(a_ref[...], w_ref[...],
                                        preferred_element_type=F32)


def _outproj(a, w, res):
    tm = PROJ_TM
    return pl.pallas_call(
        _outproj_kernel,
        grid=(TOKENS // tm,),
        in_specs=[pl.BlockSpec((tm, D_MODEL), lambda i: (i, 0)),
                  pl.BlockSpec((D_MODEL, D_MODEL), lambda i: (0, 0)),
                  pl.BlockSpec((tm, D_MODEL), lambda i: (i, 0))],
        out_specs=pl.BlockSpec((tm, D_MODEL), lambda i: (i, 0)),
        out_shape=jax.ShapeDtypeStruct((TOKENS, D_MODEL), F32),
        compiler_params=_params("parallel"),
        name="outproj",
    )(a, w, res)


def _gla_inproj_kernel(x_ref, g_ref, wq_ref, wk_ref, wv_ref, wr_ref, wz_ref,
                       wg_ref, bg_ref, q_ref, k_ref, v_ref, r_ref, gate_ref):
    hn = _rms(x_ref[...], g_ref[...]).astype(BF16)
    q = jnp.dot(hn, wq_ref[...], preferred_element_type=F32)
    q_ref[...] = q * GLA_SCALE
    k_ref[...] = jnp.dot(hn, wk_ref[...], preferred_element_type=F32)
    v_ref[...] = jnp.dot(hn, wv_ref[...], preferred_element_type=F32).astype(BF16)
    r_ref[...] = jnp.dot(hn, wr_ref[...], preferred_element_type=F32)
    z = jnp.dot(hn, wz_ref[...], preferred_element_type=F32).astype(BF16)
    zg = jnp.dot(z, wg_ref[...], preferred_element_type=F32) + bg_ref[...]
    gate_ref[...] = _log_sigmoid(zg) / GLA_GATE_TAU


def _gla_inproj(h, g, wq, wk, wv, wr, wz, wg, bg):
    tm = PROJ_TM
    full = lambda i: (0, 0)
    rows = lambda i: (i, 0)
    return pl.pallas_call(
        _gla_inproj_kernel,
        grid=(TOKENS // tm,),
        in_specs=[
            pl.BlockSpec((tm, D_MODEL), rows),
            pl.BlockSpec((1, D_MODEL), full),
            pl.BlockSpec((D_MODEL, GLA_KEY_DIM), full),
            pl.BlockSpec((D_MODEL, GLA_KEY_DIM), full),
            pl.BlockSpec((D_MODEL, D_MODEL), full),
            pl.BlockSpec((D_MODEL, D_MODEL), full),
            pl.BlockSpec((D_MODEL, LANES), full),
            pl.BlockSpec((LANES, GLA_KEY_DIM), full),
            pl.BlockSpec((1, GLA_KEY_DIM), full),
        ],
        out_specs=[
            pl.BlockSpec((tm, GLA_KEY_DIM), rows),
            pl.BlockSpec((tm, GLA_KEY_DIM), rows),
            pl.BlockSpec((tm, D_MODEL), rows),
            pl.BlockSpec((tm, D_MODEL), rows),
            pl.BlockSpec((tm, GLA_KEY_DIM), rows),
        ],
        out_shape=[
            jax.ShapeDtypeStruct((TOKENS, GLA_KEY_DIM), F32),
            jax.ShapeDtypeStruct((TOKENS, GLA_KEY_DIM), F32),
            jax.ShapeDtypeStruct((TOKENS, D_MODEL), BF16),
            jax.ShapeDtypeStruct((TOKENS, D_MODEL), F32),
            jax.ShapeDtypeStruct((TOKENS, GLA_KEY_DIM), F32),
        ],
        compiler_params=_params("parallel"),
        name="gla_inproj",
    )(h, g, wq, wk, wv, wr, wz, wg, bg)


GLA_LEVELS = (1, 2, 4, 8, 16, 32)


def _gla_selection_matrix():
    c = CHUNK
    t = np.arange(c)[:, None]
    r = np.arange(c)[None, :]
    mats = []
    for m in GLA_LEVELS[1:]:
        mats.append((r > m * (t // m)) & (r <= t))
    mats.append(r <= t)
    for m in GLA_LEVELS + (c,):
        nxt = np.minimum(m * (t // m + 1), c - 1)
        mats.append((r > t) & (r <= nxt))
    return np.concatenate(mats, axis=0).astype(np.float32)


GLA_SEL_ROWS = (len(GLA_LEVELS) - 1 + 1 + len(GLA_LEVELS) + 1) * CHUNK


def _gla_chunk_kernel(q_ref, k_ref, v_ref, r_ref, gate_ref, sel_ref, gn_ref,
                      y_ref, state_ref):
    c, nh, dk, dv = CHUNK, GLA_HEADS, GLA_DK, GLA_DV

    @pl.when(pl.program_id(1) == 0)
    def _():
        state_ref[...] = jnp.zeros_like(state_ref)

    q = q_ref[...]
    k = k_ref[...]
    v = v_ref[...]
    gate = gate_ref[...]
    g_hi = gate.astype(BF16)
    g_lo = (gate - g_hi.astype(F32)).astype(BF16)
    sel = sel_ref[...]
    decay = jnp.exp(jnp.dot(sel, g_hi, preferred_element_type=F32)
                    + jnp.dot(sel, g_lo, preferred_element_type=F32))

    def dq(i):
        return decay[i * c:(i + 1) * c, :]

    n_q = len(GLA_LEVELS)
    def dk_(i):
        return decay[(n_q + i) * c:(n_q + i + 1) * c, :]

    def stack_heads(x):
        return jnp.concatenate([x[:, h * dk:(h + 1) * dk] for h in range(nh)], axis=0)

    row = lax.broadcasted_iota(jnp.int32, (nh * c, nh * c), 0)
    col = lax.broadcasted_iota(jnp.int32, (nh * c, nh * c), 1)
    x = row ^ col
    lower = row > col

    def scores(qq, kk):
        return lax.dot_general(stack_heads(qq.astype(BF16)), stack_heads(kk.astype(BF16)),
                               NT_DIMS, preferred_element_type=F32)

    attn = jnp.where(x == 0, scores(q, k), 0.0)
    for li, m in enumerate(GLA_LEVELS):
        qq = q if m == 1 else q * dq(li - 1)
        kk = k * dk_(li)
        attn = attn + jnp.where((x >= m) & (x < 2 * m) & lower, scores(qq, kk), 0.0)
    attn = attn.astype(BF16)

    qs = (q * dq(n_q - 1)).astype(BF16)
    kd = (k * dk_(len(GLA_LEVELS))).astype(BF16)
    ones = jnp.ones((c, dv), BF16)
    gn = gn_ref[...]
    for h in range(nh):
        v_h = v[:, h * dv:(h + 1) * dv]
        state = state_ref[h]
        o = jnp.dot(attn[h * c:(h + 1) * c, h * c:(h + 1) * c], v_h,
                    preferred_element_type=F32)
        o = o + jnp.dot(qs[:, h * dk:(h + 1) * dk], state.astype(BF16),
                        preferred_element_type=F32)
        b_last = (lax.dot_general(g_hi[:, h * dk:(h + 1) * dk], ones, TN_DIMS,
                                  preferred_element_type=F32)
                  + lax.dot_general(g_lo[:, h * dk:(h + 1) * dk], ones, TN_DIMS,
                                    preferred_element_type=F32))
        state_ref[h] = state * jnp.exp(b_last) + lax.dot_general(
            kd[:, h * dk:(h + 1) * dk], v_h, TN_DIMS, preferred_element_type=F32)
        o = _rms(o, gn)
        r = r_ref[:, h * dv:(h + 1) * dv]
        y_ref[:, h * dv:(h + 1) * dv] = (o * (r * jax.nn.sigmoid(r))).astype(BF16)


def _gla_chunks(q, k, v, r, gate, sel, gn):
    n = SEQ // CHUNK
    rows = lambda b, j: (b * n + j, 0)
    full = lambda b, j: (0, 0)
    return pl.pallas_call(
        _gla_chunk_kernel,
        grid=(BATCH, n),
        in_specs=[
            pl.BlockSpec((CHUNK, GLA_KEY_DIM), rows),
            pl.BlockSpec((CHUNK, GLA_KEY_DIM), rows),
            pl.BlockSpec((CHUNK, D_MODEL), rows),
            pl.BlockSpec((CHUNK, D_MODEL), rows),
            pl.BlockSpec((CHUNK, GLA_KEY_DIM), rows),
            pl.BlockSpec((GLA_SEL_ROWS, CHUNK), full),
            pl.BlockSpec((1, GLA_DV), full),
        ],
        out_specs=pl.BlockSpec((CHUNK, D_MODEL), rows),
        out_shape=jax.ShapeDtypeStruct((TOKENS, D_MODEL), BF16),
        scratch_shapes=[pltpu.VMEM((GLA_HEADS, GLA_DK, GLA_DV), F32)],
        compiler_params=_params("parallel", "arbitrary"),
        name="gla_chunks",
    )(q, k, v, r, gate, sel, gn)


def _mlp_kernel(h_ref, g_ref, wup_ref, wdn_ref, gf_ref, o_ref, hn_ref, *, final_norm):
    kf = pl.program_id(1)

    @pl.when(kf == 0)
    def _():
        x = h_ref[...]
        hn_ref[...] = _rms(x, g_ref[...]).astype(BF16)
        o_ref[...] = x

    u = jnp.dot(hn_ref[...], wup_ref[...], preferred_element_type=F32)
    u = jnp.maximum(u, 0.0)
    u = (u * u).astype(BF16)
    o_ref[...] += jnp.dot(u, wdn_ref[...], preferred_element_type=F32)

    if final_norm:
        @pl.when(kf == pl.num_programs(1) - 1)
        def _():
            o_ref[...] = _rms(o_ref[...], gf_ref[...])


def _mlp(h, g, wup, wdn, gf, final_norm):
    tm, tf = MLP_TM, MLP_TF
    return pl.pallas_call(
        functools.partial(_mlp_kernel, final_norm=final_norm),
        grid=(TOKENS // tm, D_FF // tf),
        in_specs=[
            pl.BlockSpec((tm, D_MODEL), lambda i, f: (i, 0)),
            pl.BlockSpec((1, D_MODEL), lambda i, f: (0, 0)),
            pl.BlockSpec((D_MODEL, tf), lambda i, f: (0, f)),
            pl.BlockSpec((tf, D_MODEL), lambda i, f: (f, 0)),
            pl.BlockSpec((1, D_MODEL), lambda i, f: (0, 0)),
        ],
        out_specs=pl.BlockSpec((tm, D_MODEL), lambda i, f: (i, 0)),
        out_shape=jax.ShapeDtypeStruct((TOKENS, D_MODEL), F32),
        scratch_shapes=[pltpu.VMEM((tm, D_MODEL), BF16)],
        compiler_params=_params("parallel", "arbitrary"),
        name="mlp",
    )(h, g, wup, wdn, gf)


def _row(v):
    return v.reshape(1, -1).astype(F32)


def _fox_layer(h, g, w_in, b_f, w_out):
    d = D_MODEL
    wq = w_in[:, :d].astype(BF16)
    wk = w_in[:, d:2 * d].astype(BF16)
    wvt = w_in[:, 2 * d:3 * d].T.astype(BF16)
    wf = jnp.repeat(w_in[:, 3 * d:], FOX_GATE_LANES, axis=1).astype(BF16)
    bf = _row(jnp.repeat(b_f, FOX_GATE_LANES))
    q, k, vt, f = _fox_inproj(h, _row(g), wq, wk, wvt, wf)
    cq, ck = _fox_gate(f, bf)
    o = _fox_attn(q, k, cq, ck, vt)
    return _outproj(o, w_out.astype(BF16), h)


def _gla_layer(h, g, w_in, w_gate_up, b_gate, g_norm, w_out, sel):
    kd, d = GLA_KEY_DIM, D_MODEL
    wq = w_in[:, :kd].astype(BF16)
    wk = w_in[:, kd:2 * kd].astype(BF16)
    wv = w_in[:, 2 * kd:2 * kd + d].astype(BF16)
    z0 = 2 * kd + d
    wz = jnp.pad(w_in[:, z0:z0 + GLA_GATE_RANK],
                 ((0, 0), (0, LANES - GLA_GATE_RANK))).astype(BF16)
    wr = w_in[:, z0 + GLA_GATE_RANK:].astype(BF16)
    wg = jnp.pad(w_gate_up, ((0, LANES - GLA_GATE_RANK), (0, 0))).astype(BF16)
    q, k, v, r, gate = _gla_inproj(h, _row(g), wq, wk, wv, wr, wz, wg, _row(b_gate))
    y = _gla_chunks(q, k, v, r, gate, sel, _row(g_norm))
    return _outproj(y, w_out.astype(BF16), h)


def kernel(x, fox_w_in, fox_b_f, fox_w_out, gla_w_in, gla_w_gate_up, gla_b_gate,
           gla_norm_g, gla_w_out, mlp_w_up, mlp_w_down, norm_mix_g, norm_mlp_g,
           norm_final_g):
    assert x.shape == (BATCH, SEQ, D_MODEL) and x.dtype == F32
    sel = jnp.asarray(_gla_selection_matrix(), BF16)
    h = x.reshape(TOKENS, D_MODEL)
    for i in range(DEPTH):
        j = i // 2
        if i % 2 == 0:
            h = _fox_layer(h, norm_mix_g[i], fox_w_in[j], fox_b_f[j], fox_w_out[j])
        else:
            h = _gla_layer(h, norm_mix_g[i], gla_w_in[j], gla_w_gate_up[j],
                           gla_b_gate[j], gla_norm_g[j], gla_w_out[j], sel)
        h = _mlp(h, _row(norm_mlp_g[i]), mlp_w_up[i].astype(BF16),
                 mlp_w_down[i].astype(BF16), _row(norm_final_g),
                 final_norm=(i == DEPTH - 1))
    return h.reshape(BATCH, SEQ, D_MODEL)
```

```python
import functools

import numpy as np
import jax
import jax.numpy as jnp
from jax import lax
from jax.experimental import pallas as pl
from jax.experimental.pallas import tpu as pltpu

F32 = jnp.float32
BF16 = jnp.bfloat16

D_MODEL = 1024
BATCH = 4
SEQ = 4096
TOKENS = BATCH * SEQ
DEPTH = 4
EPS = 1e-6

FOX_HEADS = 16
FOX_HEAD_DIM = D_MODEL // FOX_HEADS
FOX_SCALE = FOX_HEAD_DIM ** -0.5
FOX_GATE_LANES = 8
GLA_HEADS = 4
GLA_KEY_DIM = D_MODEL // 2
GLA_DK = GLA_KEY_DIM // GLA_HEADS
GLA_DV = D_MODEL // GLA_HEADS
GLA_GATE_RANK = 16
GLA_GATE_TAU = 16.0
GLA_SCALE = GLA_DK ** -0.5
CHUNK = 64
D_FF = 4 * D_MODEL

LANES = 128
VMEM_LIMIT = 48 * 1024 * 1024

PROJ_TM = 512
MLP_TM = 1024
MLP_TF = 512
ATT_TQ = 2048
ATT_TK = 512
ATT_KV_UNROLL = 2
assert ATT_TQ % (ATT_TK * ATT_KV_UNROLL) == 0
ATT_STRIP = 512
ATT_ONES_ROWS = 16
LOG2E = 1.4426950408889634
GATE_BLOCK = 256

NT_DIMS = (((1,), (1,)), ((), ()))
TN_DIMS = (((0,), (0,)), ((), ()))


def _params(*semantics):
    return pltpu.CompilerParams(dimension_semantics=semantics,
                                vmem_limit_bytes=VMEM_LIMIT)


def _rms(x, g):
    ms = jnp.mean(x * x, axis=-1, keepdims=True)
    return x * lax.rsqrt(ms + EPS) * g


def _log_sigmoid(x):
    return jnp.minimum(x, 0.0) - jnp.log1p(jnp.exp(-jnp.abs(x)))


def _split3(x):
    hi = x.astype(BF16)
    r1 = x - hi.astype(F32)
    mid = r1.astype(BF16)
    lo = (r1 - mid.astype(F32)).astype(BF16)
    return hi, mid, lo


def _fox_inproj_kernel(x_ref, g_ref, wq_ref, wk_ref, wvt_ref, wf_ref,
                       q_ref, k_ref, vt_ref, f_ref):
    hn = _rms(x_ref[...], g_ref[...]).astype(BF16)
    q = jnp.dot(hn, wq_ref[...], preferred_element_type=F32)
    q_ref[...] = (q * (FOX_SCALE * LOG2E)).astype(BF16)
    k_ref[...] = jnp.dot(hn, wk_ref[...], preferred_element_type=F32).astype(BF16)
    vt = lax.dot_general(wvt_ref[...], hn, NT_DIMS, preferred_element_type=F32)
    vt_ref[0] = vt.astype(BF16)
    f_ref[...] = jnp.dot(hn, wf_ref[...], preferred_element_type=F32)


def _fox_inproj(h, g, wq, wk, wvt, wf):
    tm = PROJ_TM
    per_b = SEQ // tm
    full = lambda i: (0, 0)
    return pl.pallas_call(
        _fox_inproj_kernel,
        grid=(TOKENS // tm,),
        in_specs=[
            pl.BlockSpec((tm, D_MODEL), lambda i: (i, 0)),
            pl.BlockSpec((1, D_MODEL), full),
            pl.BlockSpec((D_MODEL, D_MODEL), full),
            pl.BlockSpec((D_MODEL, D_MODEL), full),
            pl.BlockSpec((D_MODEL, D_MODEL), full),
            pl.BlockSpec((D_MODEL, LANES), full),
        ],
        out_specs=[
            pl.BlockSpec((tm, D_MODEL), lambda i: (i, 0)),
            pl.BlockSpec((tm, D_MODEL), lambda i: (i, 0)),
            pl.BlockSpec((1, D_MODEL, tm), lambda i: (i // per_b, 0, i % per_b)),
            pl.BlockSpec((tm, LANES), lambda i: (i, 0)),
        ],
        out_shape=[
            jax.ShapeDtypeStruct((TOKENS, D_MODEL), BF16),
            jax.ShapeDtypeStruct((TOKENS, D_MODEL), BF16),
            jax.ShapeDtypeStruct((BATCH, D_MODEL, SEQ), BF16),
            jax.ShapeDtypeStruct((TOKENS, LANES), F32),
        ],
        compiler_params=_params("parallel"),
        name="fox_inproj",
    )(h, g, wq, wk, wvt, wf)


def _fox_gate_kernel(f_ref, bf_ref, cq_ref, ck_ref):
    bl = GATE_BLOCK
    row = lax.broadcasted_iota(jnp.int32, (bl, bl), 0)
    col = lax.broadcasted_iota(jnp.int32, (bl, bl), 1)
    tril = (col <= row).astype(BF16)
    j = lax.broadcasted_iota(jnp.int32, (1, LANES), 1) % FOX_GATE_LANES

    def body(i, carry):
        r0 = pl.multiple_of(i * bl, bl)
        lf = _log_sigmoid(f_ref[pl.ds(r0, bl), :] + bf_ref[...])
        hi, mid, lo = _split3(lf)
        c = carry + (jnp.dot(tril, hi, preferred_element_type=F32)
                     + jnp.dot(tril, mid, preferred_element_type=F32)
                     + jnp.dot(tril, lo, preferred_element_type=F32))
        chi, cmid, clo = (t.astype(F32) for t in _split3(c * LOG2E))
        cq = jnp.where(j == 0, chi, jnp.where(j == 1, cmid, jnp.where(
            j == 2, clo, jnp.where(j < 6, 1.0, 0.0))))
        ck = jnp.where(j < 3, 1.0, jnp.where(j == 3, -chi, jnp.where(
            j == 4, -cmid, jnp.where(j == 5, -clo, 0.0))))
        cq_ref[pl.ds(r0, bl), :] = cq.astype(BF16)
        ck_ref[pl.ds(r0, bl), :] = ck.astype(BF16)
        return c[bl - 1:bl, :]

    lax.fori_loop(0, SEQ // bl, body, jnp.zeros((1, LANES), F32))


def _fox_gate(f, bf):
    return pl.pallas_call(
        _fox_gate_kernel,
        grid=(BATCH,),
        in_specs=[pl.BlockSpec((SEQ, LANES), lambda b: (b, 0)),
                  pl.BlockSpec((1, LANES), lambda b: (0, 0))],
        out_specs=[pl.BlockSpec((SEQ, LANES), lambda b: (b, 0)),
                   pl.BlockSpec((SEQ, LANES), lambda b: (b, 0))],
        out_shape=[jax.ShapeDtypeStruct((TOKENS, LANES), BF16),
                   jax.ShapeDtypeStruct((TOKENS, LANES), BF16)],
        compiler_params=_params("parallel"),
        name="fox_gate",
    )(f, bf)


def _fox_attn_kernel(q_ref, k_ref, cq_ref, ck_ref, vt_ref, o_ref):
    pair = pl.program_id(1)
    tq, tk, hd, w = ATT_TQ, ATT_TK, FOX_HEAD_DIM, ATT_STRIP
    lane = lax.broadcasted_iota(jnp.int32, (1, LANES), 1)

    def q_body(qi, _):
        q0 = pl.multiple_of(qi * tq, tq)
        q_t = q_ref[pl.ds(q0, tq), :].astype(F32)
        cq_t = cq_ref[pl.ds(q0, tq), :].astype(F32)
        qcats = [
            jnp.concatenate(
                [jnp.where(lane // hd == hh, q_t, 0.0).astype(BF16),
                 jnp.where(lane // FOX_GATE_LANES == 2 * pair + hh, cq_t,
                           0.0).astype(BF16)], axis=1)
            for hh in range(2)]

        units = [(hh, st) for st in range(tq // w) for hh in range(2)]
        ones = jnp.ones((ATT_ONES_ROWS, tk), BF16)

        def tile(kj, carry, diag=None):
            k0 = pl.multiple_of(kj * tk, tk)
            kcat = jnp.concatenate(
                [k_ref[pl.ds(k0, tk), :], ck_ref[pl.ds(k0, tk), :]], axis=1)
            live = [u for u in range(len(units))
                    if diag is None or (units[u][1] + 1) * w > diag * tk]

            def scores(u):
                hh, st = units[u]
                s = lax.dot_general(kcat, qcats[hh][st * w:(st + 1) * w, :], NT_DIMS,
                                    preferred_element_type=F32)
                if diag is not None and (diag + 1) * tk > st * w + 1:
                    s_idx = lax.broadcasted_iota(jnp.int32, (tk, w), 0) + diag * tk
                    t_idx = lax.broadcasted_iota(jnp.int32, (tk, w), 1) + st * w
                    s = jnp.where(s_idx <= t_idx, s, -jnp.inf)
                return s

            def softmax(u, s):
                m = carry[u][0]
                m_new = jnp.maximum(m, jnp.max(s, axis=0, keepdims=True))
                return m_new, jnp.exp2(m - m_new), jnp.exp2(s - m_new).astype(BF16)

            def weighted(u, alpha, p):
                hh, _ = units[u]
                vt = jnp.concatenate(
                    [vt_ref[0, pl.ds(hh * hd, hd), pl.ds(k0, tk)], ones], axis=0)
                return alpha * carry[u][1] + jnp.dot(vt, p, preferred_element_type=F32)

            n = len(live)
            s_of, sm_of, out = {}, {}, list(carry)
            for step in range(n + 2):
                if step < n:
                    s_of[step] = scores(live[step])
                if 1 <= step <= n:
                    sm_of[step - 1] = softmax(live[step - 1], s_of.pop(step - 1))
                if step >= 2:
                    m_new, alpha, p = sm_of.pop(step - 2)
                    out[live[step - 2]] = (m_new, weighted(live[step - 2], alpha, p))
            return tuple(out)

        init = tuple((jnp.full((1, w), -jnp.inf, F32),
                      jnp.zeros((hd + ATT_ONES_ROWS, w), F32)) for _ in units)
        n_full = qi * (tq // tk)

        def full_tiles(i, carry):
            for j in range(ATT_KV_UNROLL):
                carry = tile(i * ATT_KV_UNROLL + j, carry)
            return carry

        carry = lax.fori_loop(0, n_full // ATT_KV_UNROLL, full_tiles, init)
        for d in range(tq // tk):
            carry = tile(n_full + d, carry, diag=d)
        heads = []
        for hh in range(2):
            acc = jnp.concatenate([carry[u][1] for u in range(len(units))
                                   if units[u][0] == hh], axis=1)
            heads.append(acc[:hd] / acc[hd:hd + 1])
        o = jnp.concatenate(heads, axis=0)
        o_ref[pl.ds(q0, tq), :] = o.T.astype(BF16)
        return 0

    lax.fori_loop(0, SEQ // tq, q_body, 0)


def _fox_attn(q, k, cq, ck, vt):
    pairs = FOX_HEADS // 2
    return pl.pallas_call(
        _fox_attn_kernel,
        grid=(BATCH, pairs),
        in_specs=[
            pl.BlockSpec((SEQ, LANES), lambda b, p: (b, p)),
            pl.BlockSpec((SEQ, LANES), lambda b, p: (b, p)),
            pl.BlockSpec((SEQ, LANES), lambda b, p: (b, 0)),
            pl.BlockSpec((SEQ, LANES), lambda b, p: (b, 0)),
            pl.BlockSpec((1, LANES, SEQ), lambda b, p: (b, p, 0)),
        ],
        out_specs=pl.BlockSpec((SEQ, LANES), lambda b, p: (b, p)),
        out_shape=jax.ShapeDtypeStruct((TOKENS, D_MODEL), BF16),
        compiler_params=_params("parallel", "parallel"),
        name="fox_attn",
    )(q, k, cq, ck, vt)


def _gla_inproj_kernel(x_ref, g_ref, wq_ref, wk_ref, wv_ref, wr_ref, wz_ref,
                       wg_ref, bg_ref, q_ref, k_ref, v_ref, r_ref, gate_ref):
    hn = _rms(x_ref[...], g_ref[...]).astype(BF16)
    q = jnp.dot(hn, wq_ref[...], preferred_element_type=F32)
    q_ref[...] = q * GLA_SCALE
    k_ref[...] = jnp.dot(hn, wk_ref[...], preferred_element_type=F32)
    v_ref[...] = jnp.dot(hn, wv_ref[...], preferred_element_type=F32).astype(BF16)
    r_ref[...] = jnp.dot(hn, wr_ref[...], preferred_element_type=F32)
    z = jnp.dot(hn, wz_ref[...], preferred_element_type=F32).astype(BF16)
    zg = jnp.dot(z, wg_ref[...], preferred_element_type=F32) + bg_ref[...]
    gate_ref[...] = _log_sigmoid(zg) / GLA_GATE_TAU


def _gla_inproj(h, g, wq, wk, wv, wr, wz, wg, bg):
    tm = PROJ_TM
    full = lambda i: (0, 0)
    rows = lambda i: (i, 0)
    return pl.pallas_call(
        _gla_inproj_kernel,
        grid=(TOKENS // tm,),
        in_specs=[
            pl.BlockSpec((tm, D_MODEL), rows),
            pl.BlockSpec((1, D_MODEL), full),
            pl.BlockSpec((D_MODEL, GLA_KEY_DIM), full),
            pl.BlockSpec((D_MODEL, GLA_KEY_DIM), full),
            pl.BlockSpec((D_MODEL, D_MODEL), full),
            pl.BlockSpec((D_MODEL, D_MODEL), full),
            pl.BlockSpec((D_MODEL, LANES), full),
            pl.BlockSpec((LANES, GLA_KEY_DIM), full),
            pl.BlockSpec((1, GLA_KEY_DIM), full),
        ],
        out_specs=[
            pl.BlockSpec((tm, GLA_KEY_DIM), rows),
            pl.BlockSpec((tm, GLA_KEY_DIM), rows),
            pl.BlockSpec((tm, D_MODEL), rows),
            pl.BlockSpec((tm, D_MODEL), rows),
            pl.BlockSpec((tm, GLA_KEY_DIM), rows),
        ],
        out_shape=[
            jax.ShapeDtypeStruct((TOKENS, GLA_KEY_DIM), F32),
            jax.ShapeDtypeStruct((TOKENS, GLA_KEY_DIM), F32),
            jax.ShapeDtypeStruct((TOKENS, D_MODEL), BF16),
            jax.ShapeDtypeStruct((TOKENS, D_MODEL), F32),
            jax.ShapeDtypeStruct((TOKENS, GLA_KEY_DIM), F32),
        ],
        compiler_params=_params("parallel"),
        name="gla_inproj",
    )(h, g, wq, wk, wv, wr, wz, wg, bg)


GLA_LEVELS = (1, 2, 4, 8, 16, 32)
GLA_STEP_CHUNKS = 4


def _gla_selection_matrix():
    c = CHUNK
    t = np.arange(c)[:, None]
    r = np.arange(c)[None, :]
    mats = []
    for m in GLA_LEVELS[1:]:
        mats.append((r > m * (t // m)) & (r <= t))
    mats.append(r <= t)
    for m in GLA_LEVELS + (c,):
        nxt = np.minimum(m * (t // m + 1), c - 1)
        mats.append((r > t) & (r <= nxt))
    sel = np.concatenate(mats, axis=0).astype(np.float32)
    return np.concatenate([sel, sel], axis=1)


GLA_SEL_ROWS = (len(GLA_LEVELS) - 1 + 1 + len(GLA_LEVELS) + 1) * CHUNK


def _gla_chunk_kernel(q_ref, k_ref, v_ref, r_ref, gate_ref, sel_ref, gn_ref,
                      y_ref, state_ref):
    c, nh, dk, dv = CHUNK, GLA_HEADS, GLA_DK, GLA_DV

    @pl.when(pl.program_id(1) == 0)
    def _():
        state_ref[...] = jnp.zeros_like(state_ref)

    n_q = len(GLA_LEVELS)
    row = lax.broadcasted_iota(jnp.int32, (nh * c, nh * c), 0)
    col = lax.broadcasted_iota(jnp.int32, (nh * c, nh * c), 1)
    x = row ^ col
    lower = row > col

    def stack_heads(a):
        return jnp.concatenate([a[:, h * dk:(h + 1) * dk] for h in range(nh)], axis=0)

    def scores(qq, kk):
        return lax.dot_general(stack_heads(qq.astype(BF16)), stack_heads(kk.astype(BF16)),
                               NT_DIMS, preferred_element_type=F32)

    def state_free_part(rows):
        q = q_ref[rows, :]
        k = k_ref[rows, :]
        gate = gate_ref[rows, :]
        g_hi = gate.astype(BF16)
        g_lo = (gate - g_hi.astype(F32)).astype(BF16)
        decay = jnp.exp(jnp.dot(sel_ref[...], jnp.concatenate([g_hi, g_lo], axis=0),
                                preferred_element_type=F32))
        dq = lambda i: decay[i * c:(i + 1) * c, :]
        dk_ = lambda i: decay[(n_q + i) * c:(n_q + i + 1) * c, :]
        attn = jnp.where(x == 0, scores(q, k), 0.0)
        for li, m in enumerate(GLA_LEVELS):
            qq = q if m == 1 else q * dq(li - 1)
            attn = attn + jnp.where((x >= m) & (x < 2 * m) & lower,
                                    scores(qq, k * dk_(li)), 0.0)
        e_q = dq(n_q - 1)
        qs = (q * e_q).astype(BF16)
        kd = (k * dk_(len(GLA_LEVELS))).astype(BF16)
        return attn.astype(BF16), qs, kd, e_q[c - 1:c, :]

    chunks = [pl.ds(ci * c, c) for ci in range(GLA_STEP_CHUNKS)]
    parts = [state_free_part(rows) for rows in chunks]
    gn = gn_ref[...]
    for rows, (attn, qs, kd, e_last) in zip(chunks, parts):
        for h in range(nh):
            v_h = v_ref[rows, h * dv:(h + 1) * dv]
            state_t = state_ref[h]
            o = jnp.dot(attn[h * c:(h + 1) * c, h * c:(h + 1) * c], v_h,
                        preferred_element_type=F32)
            o = o + lax.dot_general(qs[:, h * dk:(h + 1) * dk], state_t.astype(BF16),
                                    NT_DIMS, preferred_element_type=F32)
            state_ref[h] = state_t * e_last[:, h * dk:(h + 1) * dk] + lax.dot_general(
                v_h, kd[:, h * dk:(h + 1) * dk], TN_DIMS, preferred_element_type=F32)
            o = _rms(o, gn)
            r = r_ref[rows, h * dv:(h + 1) * dv]
            y_ref[rows, h * dv:(h + 1) * dv] = (o * (r * jax.nn.sigmoid(r))).astype(BF16)


def _gla_chunks(q, k, v, r, gate, sel, gn):
    tm = GLA_STEP_CHUNKS * CHUNK
    n = SEQ // tm
    rows = lambda b, j: (b * n + j, 0)
    full = lambda b, j: (0, 0)
    return pl.pallas_call(
        _gla_chunk_kernel,
        grid=(BATCH, n),
        in_specs=[
            pl.BlockSpec((tm, GLA_KEY_DIM), rows),
            pl.BlockSpec((tm, GLA_KEY_DIM), rows),
            pl.BlockSpec((tm, D_MODEL), rows),
            pl.BlockSpec((tm, D_MODEL), rows),
            pl.BlockSpec((tm, GLA_KEY_DIM), rows),
            pl.BlockSpec((GLA_SEL_ROWS, 2 * CHUNK), full),
            pl.BlockSpec((1, GLA_DV), full),
        ],
        out_specs=pl.BlockSpec((tm, D_MODEL), rows),
        out_shape=jax.ShapeDtypeStruct((TOKENS, D_MODEL), BF16),
        scratch_shapes=[pltpu.VMEM((GLA_HEADS, GLA_DV, GLA_DK), F32)],
        compiler_params=_params("parallel", "arbitrary"),
        name="gla_chunks",
    )(q, k, v, r, gate, sel, gn)


def _mlp_kernel(a_ref, wo_ref, res_ref, g_ref, wup_ref, wdn_ref, gf_ref, o_ref, hn_ref,
                *, final_norm):
    kf = pl.program_id(1)

    @pl.when(kf == 0)
    def _():
        x = res_ref[...] + jnp.dot(a_ref[...], wo_ref[...], preferred_element_type=F32)
        hn_ref[...] = _rms(x, g_ref[...]).astype(BF16)
        o_ref[...] = x

    u = jnp.dot(hn_ref[...], wup_ref[...], preferred_element_type=F32)
    u = jnp.maximum(u, 0.0)
    u = (u * u).astype(BF16)
    o_ref[...] += jnp.dot(u, wdn_ref[...], preferred_element_type=F32)

    if final_norm:
        @pl.when(kf == pl.num_programs(1) - 1)
        def _():
            o_ref[...] = _rms(o_ref[...], gf_ref[...])


def _mlp(a, w_out, jo, res, g, w_up, w_down, layer, gf, final_norm):
    tm, tf = MLP_TM, MLP_TF
    return pl.pallas_call(
        functools.partial(_mlp_kernel, final_norm=final_norm),
        grid=(TOKENS // tm, D_FF // tf),
        in_specs=[
            pl.BlockSpec((tm, D_MODEL), lambda i, f: (i, 0)),
            pl.BlockSpec((None, D_MODEL, D_MODEL), lambda i, f: (jo, 0, 0)),
            pl.BlockSpec((tm, D_MODEL), lambda i, f: (i, 0)),
            pl.BlockSpec((1, D_MODEL), lambda i, f: (0, 0)),
            pl.BlockSpec((None, D_MODEL, tf), lambda i, f: (layer, 0, f)),
            pl.BlockSpec((None, tf, D_MODEL), lambda i, f: (layer, f, 0)),
            pl.BlockSpec((1, D_MODEL), lambda i, f: (0, 0)),
        ],
        out_specs=pl.BlockSpec((tm, D_MODEL), lambda i, f: (i, 0)),
        out_shape=jax.ShapeDtypeStruct((TOKENS, D_MODEL), F32),
        scratch_shapes=[pltpu.VMEM((tm, D_MODEL), BF16)],
        compiler_params=_params("parallel", "arbitrary"),
        name="mlp",
    )(a, w_out, res, g, w_up, w_down, gf)


def _row(v):
    return v.reshape(1, -1).astype(F32)


def _fox_mixer(h, g, w_in, b_f):
    d = D_MODEL
    wq = w_in[:, :d].astype(BF16)
    wk = w_in[:, d:2 * d].astype(BF16)
    wvt = w_in[:, 2 * d:3 * d].T.astype(BF16)
    wf = jnp.repeat(w_in[:, 3 * d:], FOX_GATE_LANES, axis=1).astype(BF16)
    bf = _row(jnp.repeat(b_f, FOX_GATE_LANES))
    q, k, vt, f = _fox_inproj(h, _row(g), wq, wk, wvt, wf)
    cq, ck = _fox_gate(f, bf)
    return _fox_attn(q, k, cq, ck, vt)


def _gla_mixer(h, g, w_in, w_gate_up, b_gate, g_norm, sel):
    kd, d = GLA_KEY_DIM, D_MODEL
    wq = w_in[:, :kd].astype(BF16)
    wk = w_in[:, kd:2 * kd].astype(BF16)
    wv = w_in[:, 2 * kd:2 * kd + d].astype(BF16)
    z0 = 2 * kd + d
    wz = jnp.pad(w_in[:, z0:z0 + GLA_GATE_RANK],
                 ((0, 0), (0, LANES - GLA_GATE_RANK))).astype(BF16)
    wr = w_in[:, z0 + GLA_GATE_RANK:].astype(BF16)
    wg = jnp.pad(w_gate_up, ((0, LANES - GLA_GATE_RANK), (0, 0))).astype(BF16)
    q, k, v, r, gate = _gla_inproj(h, _row(g), wq, wk, wv, wr, wz, wg, _row(b_gate))
    return _gla_chunks(q, k, v, r, gate, sel, _row(g_norm))


def kernel(x, fox_w_in, fox_b_f, fox_w_out, gla_w_in, gla_w_gate_up, gla_b_gate,
           gla_norm_g, gla_w_out, mlp_w_up, mlp_w_down, norm_mix_g, norm_mlp_g,
           norm_final_g):
    assert x.shape == (BATCH, SEQ, D_MODEL) and x.dtype == F32
    sel = jnp.asarray(_gla_selection_matrix(), BF16)
    w_up, w_down = mlp_w_up.astype(BF16), mlp_w_down.astype(BF16)
    w_out = (fox_w_out.astype(BF16), gla_w_out.astype(BF16))
    h = x.reshape(TOKENS, D_MODEL)
    for i in range(DEPTH):
        j = i // 2
        if i % 2 == 0:
            a = _fox_mixer(h, norm_mix_g[i], fox_w_in[j], fox_b_f[j])
        else:
            a = _gla_mixer(h, norm_mix_g[i], gla_w_in[j], gla_w_gate_up[j],
                           gla_b_gate[j], gla_norm_g[j], sel)
        h = _mlp(a, w_out[i % 2], j, h, _row(norm_mlp_g[i]), w_up, w_down, i,
                 _row(norm_final_g), final_norm=(i == DEPTH - 1))
    return h.reshape(BATCH, SEQ, D_MODEL)
```

```python
import functools

import numpy as np
import jax
import jax.numpy as jnp
from jax import lax
from jax.experimental import pallas as pl
from jax.experimental.pallas import tpu as pltpu

F32 = jnp.float32
BF16 = jnp.bfloat16

D_MODEL = 1024
BATCH = 4
SEQ = 4096
TOKENS = BATCH * SEQ
DEPTH = 4
EPS = 1e-6

FOX_HEADS = 16
FOX_HEAD_DIM = D_MODEL // FOX_HEADS
FOX_SCALE = FOX_HEAD_DIM ** -0.5
FOX_GATE_LANES = 8
GLA_HEADS = 4
GLA_KEY_DIM = D_MODEL // 2
GLA_DK = GLA_KEY_DIM // GLA_HEADS
GLA_DV = D_MODEL // GLA_HEADS
GLA_GATE_RANK = 16
GLA_GATE_TAU = 16.0
GLA_SCALE = GLA_DK ** -0.5
CHUNK = 64
D_FF = 4 * D_MODEL

LANES = 128
VMEM_LIMIT = 48 * 1024 * 1024

PROJ_TM = 1024
MLP_TM = 1024
MLP_TF = 1024
ATT_TQ = 2048
ATT_TK = 512
ATT_KV_UNROLL = 2
ATT_SCORE_LEAD = 2
ATT_PV_LAG = 1
assert ATT_TQ % (ATT_TK * ATT_KV_UNROLL) == 0
ATT_STRIP = 512
ATT_ONES_ROWS = 16
LOG2E = 1.4426950408889634
GATE_BLOCK = 256

NT_DIMS = (((1,), (1,)), ((), ()))
TN_DIMS = (((0,), (0,)), ((), ()))


def _params(*semantics):
    return pltpu.CompilerParams(dimension_semantics=semantics,
                                vmem_limit_bytes=VMEM_LIMIT)


def _resident(shape):
    return pl.BlockSpec(shape, lambda *_: (0,) * len(shape),
                        pipeline_mode=pl.Buffered(1))


def _rms(x, g):
    ms = jnp.mean(x * x, axis=-1, keepdims=True)
    return x * lax.rsqrt(ms + EPS) * g


def _log_sigmoid(x):
    return jnp.minimum(x, 0.0) - jnp.log1p(jnp.exp(-jnp.abs(x)))


def _split3(x):
    hi = x.astype(BF16)
    r1 = x - hi.astype(F32)
    mid = r1.astype(BF16)
    lo = (r1 - mid.astype(F32)).astype(BF16)
    return hi, mid, lo


def _fox_inproj_kernel(x_ref, g_ref, wq_ref, wk_ref, wvt_ref, wf_ref,
                       q_ref, k_ref, vt_ref, f_ref):
    hn = _rms(x_ref[...], g_ref[...]).astype(BF16)
    q = jnp.dot(hn, wq_ref[...], preferred_element_type=F32)
    q_ref[...] = (q * (FOX_SCALE * LOG2E)).astype(BF16)
    k_ref[...] = jnp.dot(hn, wk_ref[...], preferred_element_type=F32).astype(BF16)
    vt = lax.dot_general(wvt_ref[...], hn, NT_DIMS, preferred_element_type=F32)
    vt_ref[0] = vt.astype(BF16)
    f_ref[...] = jnp.dot(hn, wf_ref[...], preferred_element_type=F32)


def _fox_inproj(h, g, wq, wk, wvt, wf):
    tm = PROJ_TM
    per_b = SEQ // tm
    return pl.pallas_call(
        _fox_inproj_kernel,
        grid=(TOKENS // tm,),
        in_specs=[
            pl.BlockSpec((tm, D_MODEL), lambda i: (i, 0)),
            _resident((1, D_MODEL)),
            _resident((D_MODEL, D_MODEL)),
            _resident((D_MODEL, D_MODEL)),
            _resident((D_MODEL, D_MODEL)),
            _resident((D_MODEL, LANES)),
        ],
        out_specs=[
            pl.BlockSpec((tm, D_MODEL), lambda i: (i, 0)),
            pl.BlockSpec((tm, D_MODEL), lambda i: (i, 0)),
            pl.BlockSpec((1, D_MODEL, tm), lambda i: (i // per_b, 0, i % per_b)),
            pl.BlockSpec((tm, LANES), lambda i: (i, 0)),
        ],
        out_shape=[
            jax.ShapeDtypeStruct((TOKENS, D_MODEL), BF16),
            jax.ShapeDtypeStruct((TOKENS, D_MODEL), BF16),
            jax.ShapeDtypeStruct((BATCH, D_MODEL, SEQ), BF16),
            jax.ShapeDtypeStruct((TOKENS, LANES), F32),
        ],
        compiler_params=_params("parallel"),
        name="fox_inproj",
    )(h, g, wq, wk, wvt, wf)


def _fox_gate_kernel(f_ref, bf_ref, cq_ref, ck_ref):
    bl = GATE_BLOCK
    row = lax.broadcasted_iota(jnp.int32, (bl, bl), 0)
    col = lax.broadcasted_iota(jnp.int32, (bl, bl), 1)
    tril = (col <= row).astype(BF16)
    j = lax.broadcasted_iota(jnp.int32, (1, LANES), 1) % FOX_GATE_LANES

    def body(i, carry):
        r0 = pl.multiple_of(i * bl, bl)
        lf = _log_sigmoid(f_ref[pl.ds(r0, bl), :] + bf_ref[...])
        hi, mid, lo = _split3(lf)
        c = carry + (jnp.dot(tril, hi, preferred_element_type=F32)
                     + jnp.dot(tril, mid, preferred_element_type=F32)
                     + jnp.dot(tril, lo, preferred_element_type=F32))
        chi, cmid, clo = (t.astype(F32) for t in _split3(c * LOG2E))
        cq = jnp.where(j == 0, chi, jnp.where(j == 1, cmid, jnp.where(
            j == 2, clo, jnp.where(j < 6, 1.0, 0.0))))
        ck = jnp.where(j < 3, 1.0, jnp.where(j == 3, -chi, jnp.where(
            j == 4, -cmid, jnp.where(j == 5, -clo, 0.0))))
        cq_ref[pl.ds(r0, bl), :] = cq.astype(BF16)
        ck_ref[pl.ds(r0, bl), :] = ck.astype(BF16)
        return c[bl - 1:bl, :]

    lax.fori_loop(0, SEQ // bl, body, jnp.zeros((1, LANES), F32))


def _fox_gate(f, bf):
    return pl.pallas_call(
        _fox_gate_kernel,
        grid=(BATCH,),
        in_specs=[pl.BlockSpec((SEQ, LANES), lambda b: (b, 0)),
                  pl.BlockSpec((1, LANES), lambda b: (0, 0))],
        out_specs=[pl.BlockSpec((SEQ, LANES), lambda b: (b, 0)),
                   pl.BlockSpec((SEQ, LANES), lambda b: (b, 0))],
        out_shape=[jax.ShapeDtypeStruct((TOKENS, LANES), BF16),
                   jax.ShapeDtypeStruct((TOKENS, LANES), BF16)],
        compiler_params=_params("parallel"),
        name="fox_gate",
    )(f, bf)


def _fox_attn_kernel(q_ref, k_ref, cq_ref, ck_ref, vt_ref, o_ref):
    pair = pl.program_id(1)
    tq, tk, hd, w = ATT_TQ, ATT_TK, FOX_HEAD_DIM, ATT_STRIP
    lane = lax.broadcasted_iota(jnp.int32, (1, LANES), 1)

    def q_body(qi, _):
        q0 = pl.multiple_of(qi * tq, tq)
        q_t = q_ref[pl.ds(q0, tq), :].astype(F32)
        cq_t = cq_ref[pl.ds(q0, tq), :].astype(F32)
        qcats = [
            jnp.concatenate(
                [jnp.where(lane // hd == hh, q_t, 0.0).astype(BF16),
                 jnp.where(lane // FOX_GATE_LANES == 2 * pair + hh, cq_t,
                           0.0).astype(BF16)], axis=1)
            for hh in range(2)]

        units = [(hh, st) for st in range(tq // w) for hh in range(2)]
        ones = jnp.ones((ATT_ONES_ROWS, tk), BF16)

        def tile(kj, carry, diag=None):
            k0 = pl.multiple_of(kj * tk, tk)
            kcat = jnp.concatenate(
                [k_ref[pl.ds(k0, tk), :], ck_ref[pl.ds(k0, tk), :]], axis=1)
            live = [u for u in range(len(units))
                    if diag is None or (units[u][1] + 1) * w > diag * tk]

            def scores(u):
                hh, st = units[u]
                s = lax.dot_general(kcat, qcats[hh][st * w:(st + 1) * w, :], NT_DIMS,
                                    preferred_element_type=F32)
                if diag is not None and (diag + 1) * tk > st * w + 1:
                    s_idx = lax.broadcasted_iota(jnp.int32, (tk, w), 0) + diag * tk
                    t_idx = lax.broadcasted_iota(jnp.int32, (tk, w), 1) + st * w
                    s = jnp.where(s_idx <= t_idx, s, -jnp.inf)
                return s

            def softmax(u, s):
                m = carry[u][0]
                m_new = jnp.maximum(m, jnp.max(s, axis=0, keepdims=True))
                return m_new, jnp.exp2(m - m_new), jnp.exp2(s - m_new).astype(BF16)

            def weighted(u, alpha, p):
                hh, _ = units[u]
                vt = jnp.concatenate(
                    [vt_ref[0, pl.ds(hh * hd, hd), pl.ds(k0, tk)], ones], axis=0)
                return alpha * carry[u][1] + jnp.dot(vt, p, preferred_element_type=F32)

            n = len(live)
            s_of, sm_of, out = {}, {}, list(carry)
            la, lb = ATT_SCORE_LEAD, ATT_SCORE_LEAD + ATT_PV_LAG
            for step in range(n + lb):
                if step < n:
                    s_of[step] = scores(live[step])
                if la <= step < n + la:
                    sm_of[step - la] = softmax(live[step - la], s_of.pop(step - la))
                if step >= lb:
                    m_new, alpha, p = sm_of.pop(step - lb)
                    out[live[step - lb]] = (m_new, weighted(live[step - lb], alpha, p))
            return tuple(out)

        init = tuple((jnp.full((1, w), -jnp.inf, F32),
                      jnp.zeros((hd + ATT_ONES_ROWS, w), F32)) for _ in units)
        n_full = qi * (tq // tk)

        def full_tiles(i, carry):
            for j in range(ATT_KV_UNROLL):
                carry = tile(i * ATT_KV_UNROLL + j, carry)
            return carry

        carry = lax.fori_loop(0, n_full // ATT_KV_UNROLL, full_tiles, init)
        for d in range(tq // tk):
            carry = tile(n_full + d, carry, diag=d)
        heads = []
        for hh in range(2):
            acc = jnp.concatenate([carry[u][1] for u in range(len(units))
                                   if units[u][0] == hh], axis=1)
            heads.append(acc[:hd] / acc[hd:hd + 1])
        o = jnp.concatenate(heads, axis=0)
        o_ref[pl.ds(q0, tq), :] = o.T.astype(BF16)
        return 0

    lax.fori_loop(0, SEQ // tq, q_body, 0)


def _fox_attn(q, k, cq, ck, vt):
    pairs = FOX_HEADS // 2
    return pl.pallas_call(
        _fox_attn_kernel,
        grid=(BATCH, pairs),
        in_specs=[
            pl.BlockSpec((SEQ, LANES), lambda b, p: (b, p)),
            pl.BlockSpec((SEQ, LANES), lambda b, p: (b, p)),
            pl.BlockSpec((SEQ, LANES), lambda b, p: (b, 0)),
            pl.BlockSpec((SEQ, LANES), lambda b, p: (b, 0)),
            pl.BlockSpec((1, LANES, SEQ), lambda b, p: (b, p, 0)),
        ],
        out_specs=pl.BlockSpec((SEQ, LANES), lambda b, p: (b, p)),
        out_shape=jax.ShapeDtypeStruct((TOKENS, D_MODEL), BF16),
        compiler_params=_params("parallel", "parallel"),
        name="fox_attn",
    )(q, k, cq, ck, vt)


def _gla_inproj_kernel(x_ref, g_ref, wq_ref, wk_ref, wv_ref, wr_ref, wz_ref,
                       wg_ref, bg_ref, q_ref, k_ref, v_ref, r_ref, gate_ref):
    hn = _rms(x_ref[...], g_ref[...]).astype(BF16)
    z = jnp.dot(hn, wz_ref[...], preferred_element_type=F32).astype(BF16)
    zg = jnp.dot(z, wg_ref[...], preferred_element_type=F32) + bg_ref[...]
    gate_ref[...] = _log_sigmoid(zg) * (LOG2E / GLA_GATE_TAU)
    q = jnp.dot(hn, wq_ref[...], preferred_element_type=F32)
    q_ref[...] = q * GLA_SCALE
    k_ref[...] = jnp.dot(hn, wk_ref[...], preferred_element_type=F32)
    v_ref[...] = jnp.dot(hn, wv_ref[...], preferred_element_type=F32).astype(BF16)
    r_ref[...] = jnp.dot(hn, wr_ref[...], preferred_element_type=F32)


def _gla_inproj(h, g, wq, wk, wv, wr, wz, wg, bg):
    tm = PROJ_TM
    rows = lambda i: (i, 0)
    return pl.pallas_call(
        _gla_inproj_kernel,
        grid=(TOKENS // tm,),
        in_specs=[
            pl.BlockSpec((tm, D_MODEL), rows),
            _resident((1, D_MODEL)),
            _resident((D_MODEL, GLA_KEY_DIM)),
            _resident((D_MODEL, GLA_KEY_DIM)),
            _resident((D_MODEL, D_MODEL)),
            _resident((D_MODEL, D_MODEL)),
            _resident((D_MODEL, LANES)),
            _resident((LANES, GLA_KEY_DIM)),
            _resident((1, GLA_KEY_DIM)),
        ],
        out_specs=[
            pl.BlockSpec((tm, GLA_KEY_DIM), rows),
            pl.BlockSpec((tm, GLA_KEY_DIM), rows),
            pl.BlockSpec((tm, D_MODEL), rows),
            pl.BlockSpec((tm, D_MODEL), rows),
            pl.BlockSpec((tm, GLA_KEY_DIM), rows),
        ],
        out_shape=[
            jax.ShapeDtypeStruct((TOKENS, GLA_KEY_DIM), F32),
            jax.ShapeDtypeStruct((TOKENS, GLA_KEY_DIM), F32),
            jax.ShapeDtypeStruct((TOKENS, D_MODEL), BF16),
            jax.ShapeDtypeStruct((TOKENS, D_MODEL), F32),
            jax.ShapeDtypeStruct((TOKENS, GLA_KEY_DIM), F32),
        ],
        compiler_params=_params("parallel"),
        name="gla_inproj",
    )(h, g, wq, wk, wv, wr, wz, wg, bg)


GLA_FINE_LEVELS = (1, 2, 4)
GLA_COARSE_LEVELS = (8, 16, 32)
GLA_LEVELS = GLA_FINE_LEVELS + GLA_COARSE_LEVELS
GLA_STEP_CHUNKS = 4
SUBLANES = 8
assert max(GLA_FINE_LEVELS) < SUBLANES <= min(GLA_COARSE_LEVELS)


def _gla_selection_matrix():
    c = CHUNK
    t = np.arange(c)[:, None]
    r = np.arange(c)[None, :]
    mats = [(r > m * (t // m)) & (r <= t) for m in GLA_FINE_LEVELS[1:]]
    for m in GLA_FINE_LEVELS:
        nxt = np.minimum(m * (t // m + 1), c - 1)
        mats.append((r > t) & (r <= nxt))
    mats.append(r <= t)
    sel = np.concatenate(mats, axis=0).astype(np.float32)
    return np.concatenate([sel, sel], axis=1)


GLA_SEL_ROWS = 2 * len(GLA_FINE_LEVELS) * CHUNK


def _gla_chunk_kernel(q_ref, k_ref, v_ref, r_ref, gate_ref, sel_ref, gn_ref,
                      y_ref, state_ref):
    c, nh, dk, dv = CHUNK, GLA_HEADS, GLA_DK, GLA_DV

    @pl.when(pl.program_id(1) == 0)
    def _():
        state_ref[...] = jnp.zeros_like(state_ref)

    t_idx = lax.broadcasted_iota(jnp.int32, (nh * c, LANES), 0) % c
    l_idx = lax.broadcasted_iota(jnp.int32, (nh * c, LANES), 1)
    h_idx = lax.broadcasted_iota(jnp.int32, (nh * c, LANES), 0) // c
    s_idx = l_idx % c
    x = t_idx ^ s_idx
    level = sum((x >= m).astype(jnp.int32) for m in GLA_LEVELS)
    level = jnp.where((l_idx // c == h_idx % (LANES // c)) & (s_idx <= t_idx), level, -1)

    def stack_heads(a):
        return jnp.concatenate([a[:, h * dk:(h + 1) * dk] for h in range(nh)], axis=0)

    def band(qq, kk):
        full = lax.dot_general(stack_heads(qq.astype(BF16)), stack_heads(kk.astype(BF16)),
                               NT_DIMS, preferred_element_type=F32)
        per_tile = LANES // c
        return jnp.concatenate(
            [full[h * c:(h + 1) * c, (h // per_tile) * LANES:(h // per_tile + 1) * LANES]
             for h in range(nh)], axis=0)

    def blockwise(b, m, side):
        out = []
        for j in range(c // m):
            blk = b[j * m:(j + 1) * m, :]
            if side == "q":
                out.append(blk - b[j * m:j * m + 1, :])
            else:
                nxt = min((j + 1) * m, c - 1)
                out.append(b[nxt:nxt + 1, :] - blk)
        return jnp.concatenate(out, axis=0)

    n_fine = len(GLA_FINE_LEVELS)

    def state_free_part(rows):
        q = q_ref[rows, :]
        k = k_ref[rows, :]
        gate = gate_ref[rows, :]
        g_hi = gate.astype(BF16)
        g_lo = (gate - g_hi.astype(F32)).astype(BF16)
        sums = jnp.dot(sel_ref[...], jnp.concatenate([g_hi, g_lo], axis=0),
                       preferred_element_type=F32)
        fine = jnp.exp2(sums[:(2 * n_fine - 1) * c, :])
        b = sums[(2 * n_fine - 1) * c:, :]
        attn = jnp.where(level == 0, band(q, k), 0.0)
        for li, m in enumerate(GLA_LEVELS):
            if m in GLA_FINE_LEVELS:
                qq = q if li == 0 else q * fine[(li - 1) * c:li * c, :]
                kk = k * fine[(n_fine - 1 + li) * c:(n_fine + li) * c, :]
            else:
                qq = q * jnp.exp2(blockwise(b, m, "q"))
                kk = k * jnp.exp2(blockwise(b, m, "k"))
            attn = jnp.where(level == li + 1, band(qq, kk), attn)
        e_q = jnp.exp2(b)
        qs = (q * e_q).astype(BF16)
        kd = (k * jnp.exp2(b[c - 1:c, :] - b)).astype(BF16)
        return attn.astype(BF16), qs, kd, e_q[c - 1:c, :]

    chunks = [pl.ds(ci * c, c) for ci in range(GLA_STEP_CHUNKS)]
    parts = [state_free_part(rows) for rows in chunks]
    gn = gn_ref[...]
    for rows, (attn, qs, kd, e_last) in zip(chunks, parts):
        for h in range(nh):
            v_h = v_ref[rows, h * dv:(h + 1) * dv]
            state_t = state_ref[h]
            o = jnp.dot(attn[h * c:(h + 1) * c, :],
                        jnp.concatenate([v_h] * (LANES // c), axis=0),
                        preferred_element_type=F32)
            o = o + lax.dot_general(qs[:, h * dk:(h + 1) * dk], state_t.astype(BF16),
                                    NT_DIMS, preferred_element_type=F32)
            state_ref[h] = state_t * e_last[:, h * dk:(h + 1) * dk] + lax.dot_general(
                v_h, kd[:, h * dk:(h + 1) * dk], TN_DIMS, preferred_element_type=F32)
            o = _rms(o, gn)
            r = r_ref[rows, h * dv:(h + 1) * dv]
            y_ref[rows, h * dv:(h + 1) * dv] = (o * (r * jax.nn.sigmoid(r))).astype(BF16)


def _gla_chunks(q, k, v, r, gate, sel, gn):
    tm = GLA_STEP_CHUNKS * CHUNK
    n = SEQ // tm
    rows = lambda b, j: (b * n + j, 0)
    full = lambda b, j: (0, 0)
    return pl.pallas_call(
        _gla_chunk_kernel,
        grid=(BATCH, n),
        in_specs=[
            pl.BlockSpec((tm, GLA_KEY_DIM), rows),
            pl.BlockSpec((tm, GLA_KEY_DIM), rows),
            pl.BlockSpec((tm, D_MODEL), rows),
            pl.BlockSpec((tm, D_MODEL), rows),
            pl.BlockSpec((tm, GLA_KEY_DIM), rows),
            pl.BlockSpec((GLA_SEL_ROWS, 2 * CHUNK), full),
            pl.BlockSpec((1, GLA_DV), full),
        ],
        out_specs=pl.BlockSpec((tm, D_MODEL), rows),
        out_shape=jax.ShapeDtypeStruct((TOKENS, D_MODEL), BF16),
        scratch_shapes=[pltpu.VMEM((GLA_HEADS, GLA_DV, GLA_DK), F32)],
        compiler_params=_params("parallel", "arbitrary"),
        name="gla_chunks",
    )(q, k, v, r, gate, sel, gn)


def _mlp_kernel(a_ref, wo_ref, res_ref, g_ref, wup_ref, wdn_ref, gf_ref, o_ref, hn_ref,
                *, final_norm):
    kf = pl.program_id(1)

    @pl.when(kf == 0)
    def _():
        x = res_ref[...] + jnp.dot(a_ref[...], wo_ref[...], preferred_element_type=F32)
        hn_ref[...] = _rms(x, g_ref[...]).astype(BF16)
        o_ref[...] = x

    u = jnp.dot(hn_ref[...], wup_ref[...], preferred_element_type=F32)
    u = jnp.maximum(u, 0.0)
    u = (u * u).astype(BF16)
    o_ref[...] += jnp.dot(u, wdn_ref[...], preferred_element_type=F32)

    if final_norm:
        @pl.when(kf == pl.num_programs(1) - 1)
        def _():
            o_ref[...] = _rms(o_ref[...], gf_ref[...])


def _mlp(a, w_out, jo, res, g, w_up, w_down, layer, gf, final_norm):
    tm, tf = MLP_TM, MLP_TF
    return pl.pallas_call(
        functools.partial(_mlp_kernel, final_norm=final_norm),
        grid=(TOKENS // tm, D_FF // tf),
        in_specs=[
            pl.BlockSpec((tm, D_MODEL), lambda i, f: (i, 0)),
            pl.BlockSpec((None, D_MODEL, D_MODEL), lambda i, f: (jo, 0, 0)),
            pl.BlockSpec((tm, D_MODEL), lambda i, f: (i, 0)),
            pl.BlockSpec((1, D_MODEL), lambda i, f: (0, 0)),
            pl.BlockSpec((None, D_MODEL, tf), lambda i, f: (layer, 0, f)),
            pl.BlockSpec((None, tf, D_MODEL), lambda i, f: (layer, f, 0)),
            pl.BlockSpec((1, D_MODEL), lambda i, f: (0, 0)),
        ],
        out_specs=pl.BlockSpec((tm, D_MODEL), lambda i, f: (i, 0)),
        out_shape=jax.ShapeDtypeStruct((TOKENS, D_MODEL), F32),
        scratch_shapes=[pltpu.VMEM((tm, D_MODEL), BF16)],
        compiler_params=_params("parallel", "arbitrary"),
        name="mlp",
    )(a, w_out, res, g, w_up, w_down, gf)


def _row(v):
    return v.reshape(1, -1).astype(F32)


def _fox_mixer(h, g, w_in, b_f):
    d = D_MODEL
    wq = w_in[:, :d].astype(BF16)
    wk = w_in[:, d:2 * d].astype(BF16)
    wvt = w_in[:, 2 * d:3 * d].T.astype(BF16)
    wf = jnp.repeat(w_in[:, 3 * d:], FOX_GATE_LANES, axis=1).astype(BF16)
    bf = _row(jnp.repeat(b_f, FOX_GATE_LANES))
    q, k, vt, f = _fox_inproj(h, _row(g), wq, wk, wvt, wf)
    cq, ck = _fox_gate(f, bf)
    return _fox_attn(q, k, cq, ck, vt)


def _gla_mixer(h, g, w_in, w_gate_up, b_gate, g_norm, sel):
    kd, d = GLA_KEY_DIM, D_MODEL
    wq = w_in[:, :kd].astype(BF16)
    wk = w_in[:, kd:2 * kd].astype(BF16)
    wv = w_in[:, 2 * kd:2 * kd + d].astype(BF16)
    z0 = 2 * kd + d
    wz = jnp.pad(w_in[:, z0:z0 + GLA_GATE_RANK],
                 ((0, 0), (0, LANES - GLA_GATE_RANK))).astype(BF16)
    wr = w_in[:, z0 + GLA_GATE_RANK:].astype(BF16)
    wg = jnp.pad(w_gate_up, ((0, LANES - GLA_GATE_RANK), (0, 0))).astype(BF16)
    q, k, v, r, gate = _gla_inproj(h, _row(g), wq, wk, wv, wr, wz, wg, _row(b_gate))
    return _gla_chunks(q, k, v, r, gate, sel, _row(g_norm))


def kernel(x, fox_w_in, fox_b_f, fox_w_out, gla_w_in, gla_w_gate_up, gla_b_gate,
           gla_norm_g, gla_w_out, mlp_w_up, mlp_w_down, norm_mix_g, norm_mlp_g,
           norm_final_g):
    assert x.shape == (BATCH, SEQ, D_MODEL) and x.dtype == F32
    sel = jnp.asarray(_gla_selection_matrix(), BF16)
    w_up, w_down = mlp_w_up.astype(BF16), mlp_w_down.astype(BF16)
    w_out = (fox_w_out.astype(BF16), gla_w_out.astype(BF16))
    h = x.reshape(TOKENS, D_MODEL)
    for i in range(DEPTH):
        j = i // 2
        if i % 2 == 0:
            a = _fox_mixer(h, norm_mix_g[i], fox_w_in[j], fox_b_f[j])
        else:
            a = _gla_mixer(h, norm_mix_g[i], gla_w_in[j], gla_w_gate_up[j],
                           gla_b_gate[j], gla_norm_g[j], sel)
        h = _mlp(a, w_out[i % 2], j, h, _row(norm_mlp_g[i]), w_up, w_down, i,
                 _row(norm_final_g), final_norm=(i == DEPTH - 1))
    return h.reshape(BATCH, SEQ, D_MODEL)
```

```python
import functools

import numpy as np
import jax
import jax.numpy as jnp
from jax import lax
from jax.experimental import pallas as pl
from jax.experimental.pallas import tpu as pltpu

F32 = jnp.float32
BF16 = jnp.bfloat16

D_MODEL = 1024
BATCH = 4
SEQ = 4096
TOKENS = BATCH * SEQ
DEPTH = 4
EPS = 1e-6

FOX_HEADS = 16
FOX_HEAD_DIM = D_MODEL // FOX_HEADS
FOX_SCALE = FOX_HEAD_DIM ** -0.5
FOX_GATE_LANES = 8
GLA_HEADS = 4
GLA_KEY_DIM = D_MODEL // 2
GLA_DK = GLA_KEY_DIM // GLA_HEADS
GLA_DV = D_MODEL // GLA_HEADS
GLA_GATE_RANK = 16
GLA_GATE_TAU = 16.0
GLA_SCALE = GLA_DK ** -0.5
CHUNK = 64
D_FF = 4 * D_MODEL

LANES = 128
VMEM_LIMIT = 48 * 1024 * 1024

PROJ_TM = 1024
MLP_TM = 512
MLP_TF = 1024
ATT_TQ = 2048
ATT_TK = 512
ATT_KV_UNROLL = 2
ATT_SCORE_LEAD = 2
ATT_PV_LAG = 1
assert ATT_TQ % (ATT_TK * ATT_KV_UNROLL) == 0
ATT_STRIP = 512
ATT_ONES_ROWS = 16
LOG2E = 1.4426950408889634
GATE_BLOCK = 256

NT_DIMS = (((1,), (1,)), ((), ()))
TN_DIMS = (((0,), (0,)), ((), ()))


def _params(*semantics):
    return pltpu.CompilerParams(dimension_semantics=semantics,
                                vmem_limit_bytes=VMEM_LIMIT)


def _resident(shape):
    return pl.BlockSpec(shape, lambda *_: (0,) * len(shape),
                        pipeline_mode=pl.Buffered(1))


def _rms(x, g):
    ms = jnp.mean(x * x, axis=-1, keepdims=True)
    return x * lax.rsqrt(ms + EPS) * g


def _log_sigmoid(x):
    return jnp.minimum(x, 0.0) - jnp.log1p(jnp.exp(-jnp.abs(x)))


def _split3(x):
    hi = x.astype(BF16)
    r1 = x - hi.astype(F32)
    mid = r1.astype(BF16)
    lo = (r1 - mid.astype(F32)).astype(BF16)
    return hi, mid, lo


def _fox_inproj_kernel(x_ref, g_ref, wq_ref, wk_ref, wvt_ref, wf_ref,
                       q_ref, k_ref, vt_ref, f_ref):
    hn = _rms(x_ref[...], g_ref[...]).astype(BF16)
    q = jnp.dot(hn, wq_ref[...], preferred_element_type=F32)
    q_ref[...] = (q * (FOX_SCALE * LOG2E)).astype(BF16)
    k_ref[...] = jnp.dot(hn, wk_ref[...], preferred_element_type=F32).astype(BF16)
    vt = lax.dot_general(wvt_ref[...], hn, NT_DIMS, preferred_element_type=F32)
    vt_ref[0] = vt.astype(BF16)
    f_ref[...] = jnp.dot(hn, wf_ref[...], preferred_element_type=F32)


def _fox_inproj(h, g, wq, wk, wvt, wf):
    tm = PROJ_TM
    per_b = SEQ // tm
    return pl.pallas_call(
        _fox_inproj_kernel,
        grid=(TOKENS // tm,),
        in_specs=[
            pl.BlockSpec((tm, D_MODEL), lambda i: (i, 0)),
            _resident((1, D_MODEL)),
            _resident((D_MODEL, D_MODEL)),
            _resident((D_MODEL, D_MODEL)),
            _resident((D_MODEL, D_MODEL)),
            _resident((D_MODEL, LANES)),
        ],
        out_specs=[
            pl.BlockSpec((tm, D_MODEL), lambda i: (i, 0)),
            pl.BlockSpec((tm, D_MODEL), lambda i: (i, 0)),
            pl.BlockSpec((1, D_MODEL, tm), lambda i: (i // per_b, 0, i % per_b)),
            pl.BlockSpec((tm, LANES), lambda i: (i, 0)),
        ],
        out_shape=[
            jax.ShapeDtypeStruct((TOKENS, D_MODEL), BF16),
            jax.ShapeDtypeStruct((TOKENS, D_MODEL), BF16),
            jax.ShapeDtypeStruct((BATCH, D_MODEL, SEQ), BF16),
            jax.ShapeDtypeStruct((TOKENS, LANES), F32),
        ],
        compiler_params=_params("parallel"),
        name="fox_inproj",
    )(h, g, wq, wk, wvt, wf)


def _fox_gate_kernel(f_ref, bf_ref, cq_ref, ck_ref):
    bl = GATE_BLOCK
    row = lax.broadcasted_iota(jnp.int32, (bl, bl), 0)
    col = lax.broadcasted_iota(jnp.int32, (bl, bl), 1)
    tril = (col <= row).astype(BF16)
    j = lax.broadcasted_iota(jnp.int32, (1, LANES), 1) % FOX_GATE_LANES

    def body(i, carry):
        r0 = pl.multiple_of(i * bl, bl)
        lf = _log_sigmoid(f_ref[pl.ds(r0, bl), :] + bf_ref[...])
        hi, mid, lo = _split3(lf)
        c = carry + (jnp.dot(tril, hi, preferred_element_type=F32)
                     + jnp.dot(tril, mid, preferred_element_type=F32)
                     + jnp.dot(tril, lo, preferred_element_type=F32))
        chi, cmid, clo = (t.astype(F32) for t in _split3(c * LOG2E))
        cq = jnp.where(j == 0, chi, jnp.where(j == 1, cmid, jnp.where(
            j == 2, clo, jnp.where(j < 6, 1.0, 0.0))))
        ck = jnp.where(j < 3, 1.0, jnp.where(j == 3, -chi, jnp.where(
            j == 4, -cmid, jnp.where(j == 5, -clo, 0.0))))
        cq_ref[pl.ds(r0, bl), :] = cq.astype(BF16)
        ck_ref[pl.ds(r0, bl), :] = ck.astype(BF16)
        return c[bl - 1:bl, :]

    lax.fori_loop(0, SEQ // bl, body, jnp.zeros((1, LANES), F32))


def _fox_gate(f, bf):
    return pl.pallas_call(
        _fox_gate_kernel,
        grid=(BATCH,),
        in_specs=[pl.BlockSpec((SEQ, LANES), lambda b: (b, 0)),
                  pl.BlockSpec((1, LANES), lambda b: (0, 0))],
        out_specs=[pl.BlockSpec((SEQ, LANES), lambda b: (b, 0)),
                   pl.BlockSpec((SEQ, LANES), lambda b: (b, 0))],
        out_shape=[jax.ShapeDtypeStruct((TOKENS, LANES), BF16),
                   jax.ShapeDtypeStruct((TOKENS, LANES), BF16)],
        compiler_params=_params("parallel"),
        name="fox_gate",
    )(f, bf)


def _fox_attn_kernel(q_ref, k_ref, cq_ref, ck_ref, vt_ref, o_ref):
    pair = pl.program_id(1)
    tq, tk, hd, w = ATT_TQ, ATT_TK, FOX_HEAD_DIM, ATT_STRIP
    lane = lax.broadcasted_iota(jnp.int32, (1, LANES), 1)

    def q_body(qi, _):
        q0 = pl.multiple_of(qi * tq, tq)
        q_t = q_ref[pl.ds(q0, tq), :].astype(F32)
        cq_t = cq_ref[pl.ds(q0, tq), :].astype(F32)
        qcats = [
            jnp.concatenate(
                [jnp.where(lane // hd == hh, q_t, 0.0).astype(BF16),
                 jnp.where(lane // FOX_GATE_LANES == 2 * pair + hh, cq_t,
                           0.0).astype(BF16)], axis=1)
            for hh in range(2)]

        units = [(hh, st) for st in range(tq // w) for hh in range(2)]
        ones = jnp.ones((ATT_ONES_ROWS, tk), BF16)

        def tile(kj, carry, diag=None):
            k0 = pl.multiple_of(kj * tk, tk)
            kcat = jnp.concatenate(
                [k_ref[pl.ds(k0, tk), :], ck_ref[pl.ds(k0, tk), :]], axis=1)
            live = [u for u in range(len(units))
                    if diag is None or (units[u][1] + 1) * w > diag * tk]

            def scores(u):
                hh, st = units[u]
                s = lax.dot_general(kcat, qcats[hh][st * w:(st + 1) * w, :], NT_DIMS,
                                    preferred_element_type=F32)
                if diag is not None and (diag + 1) * tk > st * w + 1:
                    s_idx = lax.broadcasted_iota(jnp.int32, (tk, w), 0) + diag * tk
                    t_idx = lax.broadcasted_iota(jnp.int32, (tk, w), 1) + st * w
                    s = jnp.where(s_idx <= t_idx, s, -jnp.inf)
                return s

            def softmax(u, s):
                m = carry[u][0]
                m_new = jnp.maximum(m, jnp.max(s, axis=0, keepdims=True))
                return m_new, jnp.exp2(m - m_new), jnp.exp2(s - m_new).astype(BF16)

            def weighted(u, alpha, p):
                hh, _ = units[u]
                vt = jnp.concatenate(
                    [vt_ref[0, pl.ds(hh * hd, hd), pl.ds(k0, tk)], ones], axis=0)
                return alpha * carry[u][1] + jnp.dot(vt, p, preferred_element_type=F32)

            n = len(live)
            s_of, sm_of, out = {}, {}, list(carry)
            la, lb = ATT_SCORE_LEAD, ATT_SCORE_LEAD + ATT_PV_LAG
            for step in range(n + lb):
                if step < n:
                    s_of[step] = scores(live[step])
                if la <= step < n + la:
                    sm_of[step - la] = softmax(live[step - la], s_of.pop(step - la))
                if step >= lb:
                    m_new, alpha, p = sm_of.pop(step - lb)
                    out[live[step - lb]] = (m_new, weighted(live[step - lb], alpha, p))
            return tuple(out)

        init = tuple((jnp.full((1, w), -jnp.inf, F32),
                      jnp.zeros((hd + ATT_ONES_ROWS, w), F32)) for _ in units)
        n_full = qi * (tq // tk)

        def full_tiles(i, carry):
            for j in range(ATT_KV_UNROLL):
                carry = tile(i * ATT_KV_UNROLL + j, carry)
            return carry

        carry = lax.fori_loop(0, n_full // ATT_KV_UNROLL, full_tiles, init)
        for d in range(tq // tk):
            carry = tile(n_full + d, carry, diag=d)
        heads = []
        for hh in range(2):
            acc = jnp.concatenate([carry[u][1] for u in range(len(units))
                                   if units[u][0] == hh], axis=1)
            heads.append(acc[:hd] / acc[hd:hd + 1])
        o = jnp.concatenate(heads, axis=0)
        o_ref[pl.ds(q0, tq), :] = o.T.astype(BF16)
        return 0

    lax.fori_loop(0, SEQ // tq, q_body, 0)


def _fox_attn(q, k, cq, ck, vt):
    pairs = FOX_HEADS // 2
    return pl.pallas_call(
        _fox_attn_kernel,
        grid=(BATCH, pairs),
        in_specs=[
            pl.BlockSpec((SEQ, LANES), lambda b, p: (b, p)),
            pl.BlockSpec((SEQ, LANES), lambda b, p: (b, p)),
            pl.BlockSpec((SEQ, LANES), lambda b, p: (b, 0)),
            pl.BlockSpec((SEQ, LANES), lambda b, p: (b, 0)),
            pl.BlockSpec((1, LANES, SEQ), lambda b, p: (b, p, 0)),
        ],
        out_specs=pl.BlockSpec((SEQ, LANES), lambda b, p: (b, p)),
        out_shape=jax.ShapeDtypeStruct((TOKENS, D_MODEL), BF16),
        compiler_params=_params("parallel", "parallel"),
        name="fox_attn",
    )(q, k, cq, ck, vt)


def _gla_inproj_kernel(x_ref, g_ref, wq_ref, wk_ref, wv_ref, wr_ref, wz_ref,
                       wg_ref, bg_ref, q_ref, k_ref, v_ref, r_ref, gate_ref):
    hn = _rms(x_ref[...], g_ref[...]).astype(BF16)
    z = jnp.dot(hn, wz_ref[...], preferred_element_type=F32).astype(BF16)
    zg = jnp.dot(z, wg_ref[...], preferred_element_type=F32) + bg_ref[...]
    gate_ref[...] = _log_sigmoid(zg) * (LOG2E / GLA_GATE_TAU)
    q = jnp.dot(hn, wq_ref[...], preferred_element_type=F32)
    q_ref[...] = q * GLA_SCALE
    k_ref[...] = jnp.dot(hn, wk_ref[...], preferred_element_type=F32)
    v_ref[...] = jnp.dot(hn, wv_ref[...], preferred_element_type=F32).astype(BF16)
    r_ref[...] = jnp.dot(hn, wr_ref[...], preferred_element_type=F32)


def _gla_inproj(h, g, wq, wk, wv, wr, wz, wg, bg):
    tm = PROJ_TM
    rows = lambda i: (i, 0)
    return pl.pallas_call(
        _gla_inproj_kernel,
        grid=(TOKENS // tm,),
        in_specs=[
            pl.BlockSpec((tm, D_MODEL), rows),
            _resident((1, D_MODEL)),
            _resident((D_MODEL, GLA_KEY_DIM)),
            _resident((D_MODEL, GLA_KEY_DIM)),
            _resident((D_MODEL, D_MODEL)),
            _resident((D_MODEL, D_MODEL)),
            _resident((D_MODEL, LANES)),
            _resident((LANES, GLA_KEY_DIM)),
            _resident((1, GLA_KEY_DIM)),
        ],
        out_specs=[
            pl.BlockSpec((tm, GLA_KEY_DIM), rows),
            pl.BlockSpec((tm, GLA_KEY_DIM), rows),
            pl.BlockSpec((tm, D_MODEL), rows),
            pl.BlockSpec((tm, D_MODEL), rows),
            pl.BlockSpec((tm, GLA_KEY_DIM), rows),
        ],
        out_shape=[
            jax.ShapeDtypeStruct((TOKENS, GLA_KEY_DIM), F32),
            jax.ShapeDtypeStruct((TOKENS, GLA_KEY_DIM), F32),
            jax.ShapeDtypeStruct((TOKENS, D_MODEL), BF16),
            jax.ShapeDtypeStruct((TOKENS, D_MODEL), F32),
            jax.ShapeDtypeStruct((TOKENS, GLA_KEY_DIM), F32),
        ],
        compiler_params=_params("parallel"),
        name="gla_inproj",
    )(h, g, wq, wk, wv, wr, wz, wg, bg)


GLA_FINE_LEVELS = (1, 2, 4)
GLA_COARSE_LEVELS = (8, 16, 32)
GLA_LEVELS = GLA_FINE_LEVELS + GLA_COARSE_LEVELS
GLA_STEP_CHUNKS = 4
SUBLANES = 8
assert max(GLA_FINE_LEVELS) < SUBLANES <= min(GLA_COARSE_LEVELS)


def _gla_selection_matrix():
    c = CHUNK
    t = np.arange(c)[:, None]
    r = np.arange(c)[None, :]
    mats = [(r > m * (t // m)) & (r <= t) for m in GLA_FINE_LEVELS[1:]]
    for m in GLA_FINE_LEVELS:
        nxt = np.minimum(m * (t // m + 1), c - 1)
        mats.append((r > t) & (r <= nxt))
    mats.append(r <= t)
    sel = np.concatenate(mats, axis=0).astype(np.float32)
    return np.concatenate([sel, sel], axis=1)


GLA_SEL_ROWS = 2 * len(GLA_FINE_LEVELS) * CHUNK


def _gla_chunk_kernel(q_ref, k_ref, v_ref, r_ref, gate_ref, sel_ref, gn_ref,
                      y_ref, state_ref):
    c, nh, dk, dv = CHUNK, GLA_HEADS, GLA_DK, GLA_DV

    @pl.when(pl.program_id(1) == 0)
    def _():
        state_ref[...] = jnp.zeros_like(state_ref)

    t_idx = lax.broadcasted_iota(jnp.int32, (nh * c, LANES), 0) % c
    l_idx = lax.broadcasted_iota(jnp.int32, (nh * c, LANES), 1)
    h_idx = lax.broadcasted_iota(jnp.int32, (nh * c, LANES), 0) // c
    s_idx = l_idx % c
    x = t_idx ^ s_idx
    level = sum((x >= m).astype(jnp.int32) for m in GLA_LEVELS)
    level = jnp.where((l_idx // c == h_idx % (LANES // c)) & (s_idx <= t_idx), level, -1)

    def stack_heads(a):
        return jnp.concatenate([a[:, h * dk:(h + 1) * dk] for h in range(nh)], axis=0)

    def band(qq, kk):
        full = lax.dot_general(stack_heads(qq.astype(BF16)), stack_heads(kk.astype(BF16)),
                               NT_DIMS, preferred_element_type=F32)
        per_tile = LANES // c
        return jnp.concatenate(
            [full[h * c:(h + 1) * c, (h // per_tile) * LANES:(h // per_tile + 1) * LANES]
             for h in range(nh)], axis=0)

    def blockwise(b, m, side):
        out = []
        for j in range(c // m):
            blk = b[j * m:(j + 1) * m, :]
            if side == "q":
                out.append(blk - b[j * m:j * m + 1, :])
            else:
                nxt = min((j + 1) * m, c - 1)
                out.append(b[nxt:nxt + 1, :] - blk)
        return jnp.concatenate(out, axis=0)

    n_fine = len(GLA_FINE_LEVELS)

    def state_free_part(rows):
        q = q_ref[rows, :]
        k = k_ref[rows, :]
        gate = gate_ref[rows, :]
        g_hi = gate.astype(BF16)
        g_lo = (gate - g_hi.astype(F32)).astype(BF16)
        sums = jnp.dot(sel_ref[...], jnp.concatenate([g_hi, g_lo], axis=0),
                       preferred_element_type=F32)
        fine = jnp.exp2(sums[:(2 * n_fine - 1) * c, :])
        b = sums[(2 * n_fine - 1) * c:, :]
        attn = jnp.where(level == 0, band(q, k), 0.0)
        for li, m in enumerate(GLA_LEVELS):
            if m in GLA_FINE_LEVELS:
                qq = q if li == 0 else q * fine[(li - 1) * c:li * c, :]
                kk = k * fine[(n_fine - 1 + li) * c:(n_fine + li) * c, :]
            else:
                qq = q * jnp.exp2(blockwise(b, m, "q"))
                kk = k * jnp.exp2(blockwise(b, m, "k"))
            attn = jnp.where(level == li + 1, band(qq, kk), attn)
        e_q = jnp.exp2(b)
        qs = (q * e_q).astype(BF16)
        kd = (k * jnp.exp2(b[c - 1:c, :] - b)).astype(BF16)
        return attn.astype(BF16), qs, kd, e_q[c - 1:c, :]

    chunks = [pl.ds(ci * c, c) for ci in range(GLA_STEP_CHUNKS)]
    parts = [state_free_part(rows) for rows in chunks]
    gn = gn_ref[...]
    for rows, (attn, qs, kd, e_last) in zip(chunks, parts):
        for h in range(nh):
            v_h = v_ref[rows, h * dv:(h + 1) * dv]
            state_t = state_ref[h]
            o = jnp.dot(attn[h * c:(h + 1) * c, :],
                        jnp.concatenate([v_h] * (LANES // c), axis=0),
                        preferred_element_type=F32)
            o = o + lax.dot_general(qs[:, h * dk:(h + 1) * dk], state_t.astype(BF16),
                                    NT_DIMS, preferred_element_type=F32)
            state_ref[h] = state_t * e_last[:, h * dk:(h + 1) * dk] + lax.dot_general(
                v_h, kd[:, h * dk:(h + 1) * dk], TN_DIMS, preferred_element_type=F32)
            o = _rms(o, gn)
            r = r_ref[rows, h * dv:(h + 1) * dv]
            y_ref[rows, h * dv:(h + 1) * dv] = (o * (r * jax.nn.sigmoid(r))).astype(BF16)


def _gla_chunks(q, k, v, r, gate, sel, gn):
    tm = GLA_STEP_CHUNKS * CHUNK
    n = SEQ // tm
    rows = lambda b, j: (b * n + j, 0)
    full = lambda b, j: (0, 0)
    return pl.pallas_call(
        _gla_chunk_kernel,
        grid=(BATCH, n),
        in_specs=[
            pl.BlockSpec((tm, GLA_KEY_DIM), rows),
            pl.BlockSpec((tm, GLA_KEY_DIM), rows),
            pl.BlockSpec((tm, D_MODEL), rows),
            pl.BlockSpec((tm, D_MODEL), rows),
            pl.BlockSpec((tm, GLA_KEY_DIM), rows),
            pl.BlockSpec((GLA_SEL_ROWS, 2 * CHUNK), full),
            pl.BlockSpec((1, GLA_DV), full),
        ],
        out_specs=pl.BlockSpec((tm, D_MODEL), rows),
        out_shape=jax.ShapeDtypeStruct((TOKENS, D_MODEL), BF16),
        scratch_shapes=[pltpu.VMEM((GLA_HEADS, GLA_DV, GLA_DK), F32)],
        compiler_params=_params("parallel", "arbitrary"),
        name="gla_chunks",
    )(q, k, v, r, gate, sel, gn)


def _mlp_kernel(a_ref, wo_ref, res_ref, g_ref, wup_ref, wdn_ref, gf_ref, o_ref,
                *, final_norm):
    x = res_ref[...] + jnp.dot(a_ref[...], wo_ref[...], preferred_element_type=F32)
    hn = _rms(x, g_ref[...]).astype(BF16)
    for f in range(D_FF // MLP_TF):
        cols = slice(f * MLP_TF, (f + 1) * MLP_TF)
        u = jnp.maximum(jnp.dot(hn, wup_ref[:, cols], preferred_element_type=F32), 0.0)
        x = x + jnp.dot((u * u).astype(BF16), wdn_ref[cols, :],
                        preferred_element_type=F32)
    o_ref[...] = _rms(x, gf_ref[...]) if final_norm else x


def _mlp(a, w_out, jo, res, g, w_up, w_down, layer, gf, final_norm):
    tm = MLP_TM
    rows = lambda i: (i, 0)

    def layer_weights(shape, idx):
        return pl.BlockSpec((None,) + shape, lambda i: (idx, 0, 0),
                            pipeline_mode=pl.Buffered(1))

    return pl.pallas_call(
        functools.partial(_mlp_kernel, final_norm=final_norm),
        grid=(TOKENS // tm,),
        in_specs=[
            pl.BlockSpec((tm, D_MODEL), rows),
            layer_weights((D_MODEL, D_MODEL), jo),
            pl.BlockSpec((tm, D_MODEL), rows),
            _resident((1, D_MODEL)),
            layer_weights((D_MODEL, D_FF), layer),
            layer_weights((D_FF, D_MODEL), layer),
            _resident((1, D_MODEL)),
        ],
        out_specs=pl.BlockSpec((tm, D_MODEL), rows),
        out_shape=jax.ShapeDtypeStruct((TOKENS, D_MODEL), F32),
        compiler_params=_params("parallel"),
        name="mlp",
    )(a, w_out, res, g, w_up, w_down, gf)


def _row(v):
    return v.reshape(1, -1).astype(F32)


def _fox_mixer(h, g, w_in, b_f):
    d = D_MODEL
    wq = w_in[:, :d].astype(BF16)
    wk = w_in[:, d:2 * d].astype(BF16)
    wvt = w_in[:, 2 * d:3 * d].T.astype(BF16)
    wf = jnp.repeat(w_in[:, 3 * d:], FOX_GATE_LANES, axis=1).astype(BF16)
    bf = _row(jnp.repeat(b_f, FOX_GATE_LANES))
    q, k, vt, f = _fox_inproj(h, _row(g), wq, wk, wvt, wf)
    cq, ck = _fox_gate(f, bf)
    return _fox_attn(q, k, cq, ck, vt)


def _gla_mixer(h, g, w_in, w_gate_up, b_gate, g_norm, sel):
    kd, d = GLA_KEY_DIM, D_MODEL
    wq = w_in[:, :kd].astype(BF16)
    wk = w_in[:, kd:2 * kd].astype(BF16)
    wv = w_in[:, 2 * kd:2 * kd + d].astype(BF16)
    z0 = 2 * kd + d
    wz = jnp.pad(w_in[:, z0:z0 + GLA_GATE_RANK],
                 ((0, 0), (0, LANES - GLA_GATE_RANK))).astype(BF16)
    wr = w_in[:, z0 + GLA_GATE_RANK:].astype(BF16)
    wg = jnp.pad(w_gate_up, ((0, LANES - GLA_GATE_RANK), (0, 0))).astype(BF16)
    q, k, v, r, gate = _gla_inproj(h, _row(g), wq, wk, wv, wr, wz, wg, _row(b_gate))
    return _gla_chunks(q, k, v, r, gate, sel, _row(g_norm))


def kernel(x, fox_w_in, fox_b_f, fox_w_out, gla_w_in, gla_w_gate_up, gla_b_gate,
           gla_norm_g, gla_w_out, mlp_w_up, mlp_w_down, norm_mix_g, norm_mlp_g,
           norm_final_g):
    assert x.shape == (BATCH, SEQ, D_MODEL) and x.dtype == F32
    sel = jnp.asarray(_gla_selection_matrix(), BF16)
    w_up, w_down = mlp_w_up.astype(BF16), mlp_w_down.astype(BF16)
    w_out = (fox_w_out.astype(BF16), gla_w_out.astype(BF16))
    h = x.reshape(TOKENS, D_MODEL)
    for i in range(DEPTH):
        j = i // 2
        if i % 2 == 0:
            a = _fox_mixer(h, norm_mix_g[i], fox_w_in[j], fox_b_f[j])
        else:
            a = _gla_mixer(h, norm_mix_g[i], gla_w_in[j], gla_w_gate_up[j],
                           gla_b_gate[j], gla_norm_g[j], sel)
        h = _mlp(a, w_out[i % 2], j, h, _row(norm_mlp_g[i]), w_up, w_down, i,
                 _row(norm_final_g), final_norm=(i == DEPTH - 1))
    return h.reshape(BATCH, SEQ, D_MODEL)
```

```python
import functools

import numpy as np
import jax
import jax.numpy as jnp
from jax import lax
from jax.experimental import pallas as pl
from jax.experimental.pallas import tpu as pltpu

F32 = jnp.float32
BF16 = jnp.bfloat16

D_MODEL = 1024
BATCH = 4
SEQ = 4096
TOKENS = BATCH * SEQ
DEPTH = 4
EPS = 1e-6

FOX_HEADS = 16
FOX_HEAD_DIM = D_MODEL // FOX_HEADS
FOX_SCALE = FOX_HEAD_DIM ** -0.5
FOX_GATE_LANES = 8
FOX_BIAS_TERMS = 3
GLA_HEADS = 4
GLA_KEY_DIM = D_MODEL // 2
GLA_DK = GLA_KEY_DIM // GLA_HEADS
GLA_DV = D_MODEL // GLA_HEADS
GLA_GATE_RANK = 16
GLA_GATE_TAU = 16.0
GLA_SCALE = GLA_DK ** -0.5
CHUNK = 64
D_FF = 4 * D_MODEL

LANES = 128
VMEM_LIMIT = 48 * 1024 * 1024

PROJ_TM = 1024
MLP_TM = 512
MLP_TF = 1024
ATT_TQ = 2048
ATT_TK = 512
ATT_KV_UNROLL = 2
ATT_SCORE_LEAD = 2
ATT_PV_LAG = 1
assert ATT_TQ % (ATT_TK * ATT_KV_UNROLL) == 0
ATT_STRIP = 512
ATT_ONES_ROWS = 16
LOG2E = 1.4426950408889634
GATE_BLOCK = 256

NT_DIMS = (((1,), (1,)), ((), ()))
TN_DIMS = (((0,), (0,)), ((), ()))


def _params(*semantics):
    return pltpu.CompilerParams(dimension_semantics=semantics,
                                vmem_limit_bytes=VMEM_LIMIT)


def _resident(shape):
    return pl.BlockSpec(shape, lambda *_: (0,) * len(shape),
                        pipeline_mode=pl.Buffered(1))


def _rms(x, g):
    ms = jnp.mean(x * x, axis=-1, keepdims=True)
    return x * lax.rsqrt(ms + EPS) * g


def _log_sigmoid(x):
    return jnp.minimum(x, 0.0) - jnp.log1p(jnp.exp(-jnp.abs(x)))


def _split3(x):
    hi = x.astype(BF16)
    r1 = x - hi.astype(F32)
    mid = r1.astype(BF16)
    lo = (r1 - mid.astype(F32)).astype(BF16)
    return hi, mid, lo


def _fox_inproj_kernel(x_ref, g_ref, wqt_ref, wk_ref, wvt_ref, wf_ref,
                       qt_ref, k_ref, vt_ref, f_ref):
    hn = _rms(x_ref[...], g_ref[...]).astype(BF16)
    qt = lax.dot_general(wqt_ref[...], hn, NT_DIMS, preferred_element_type=F32)
    qt_ref[0] = (qt * (FOX_SCALE * LOG2E)).astype(BF16)
    k_ref[...] = jnp.dot(hn, wk_ref[...], preferred_element_type=F32).astype(BF16)
    vt = lax.dot_general(wvt_ref[...], hn, NT_DIMS, preferred_element_type=F32)
    vt_ref[0] = vt.astype(BF16)
    f_ref[...] = jnp.dot(hn, wf_ref[...], preferred_element_type=F32)


def _fox_inproj(h, g, wqt, wk, wvt, wf):
    tm = PROJ_TM
    per_b = SEQ // tm
    return pl.pallas_call(
        _fox_inproj_kernel,
        grid=(TOKENS // tm,),
        in_specs=[
            pl.BlockSpec((tm, D_MODEL), lambda i: (i, 0)),
            _resident((1, D_MODEL)),
            _resident((D_MODEL, D_MODEL)),
            _resident((D_MODEL, D_MODEL)),
            _resident((D_MODEL, D_MODEL)),
            _resident((D_MODEL, LANES)),
        ],
        out_specs=[
            pl.BlockSpec((1, D_MODEL, tm), lambda i: (i // per_b, 0, i % per_b)),
            pl.BlockSpec((tm, D_MODEL), lambda i: (i, 0)),
            pl.BlockSpec((1, D_MODEL, tm), lambda i: (i // per_b, 0, i % per_b)),
            pl.BlockSpec((tm, LANES), lambda i: (i, 0)),
        ],
        out_shape=[
            jax.ShapeDtypeStruct((BATCH, D_MODEL, SEQ), BF16),
            jax.ShapeDtypeStruct((TOKENS, D_MODEL), BF16),
            jax.ShapeDtypeStruct((BATCH, D_MODEL, SEQ), BF16),
            jax.ShapeDtypeStruct((TOKENS, LANES), F32),
        ],
        compiler_params=_params("parallel"),
        name="fox_inproj",
    )(h, g, wqt, wk, wvt, wf)


def _fox_gate_kernel(f_ref, bf_ref, ck_ref):
    bl = GATE_BLOCK
    row = lax.broadcasted_iota(jnp.int32, (bl, bl), 0)
    col = lax.broadcasted_iota(jnp.int32, (bl, bl), 1)
    tril = (col <= row).astype(BF16)
    j = lax.broadcasted_iota(jnp.int32, (1, LANES), 1) % FOX_GATE_LANES

    def body(i, carry):
        r0 = pl.multiple_of(i * bl, bl)
        lf = _log_sigmoid(f_ref[pl.ds(r0, bl), :] + bf_ref[...])
        hi, mid, lo = _split3(lf)
        c = carry + (jnp.dot(tril, hi, preferred_element_type=F32)
                     + jnp.dot(tril, mid, preferred_element_type=F32)
                     + jnp.dot(tril, lo, preferred_element_type=F32))
        chi, cmid, clo = (t.astype(F32) for t in _split3(c * LOG2E))
        ck = jnp.where(j == 0, -chi, jnp.where(j == 1, -cmid, jnp.where(
            j == 2, -clo, 0.0)))
        ck_ref[pl.ds(r0, bl), :] = ck.astype(BF16)
        return c[bl - 1:bl, :]

    lax.fori_loop(0, SEQ // bl, body, jnp.zeros((1, LANES), F32))


def _fox_gate(f, bf):
    return pl.pallas_call(
        _fox_gate_kernel,
        grid=(BATCH,),
        in_specs=[pl.BlockSpec((SEQ, LANES), lambda b: (b, 0)),
                  pl.BlockSpec((1, LANES), lambda b: (0, 0))],
        out_specs=pl.BlockSpec((SEQ, LANES), lambda b: (b, 0)),
        out_shape=jax.ShapeDtypeStruct((TOKENS, LANES), BF16),
        compiler_params=_params("parallel"),
        name="fox_gate",
    )(f, bf)


def _fox_attn_kernel(qt_ref, k_ref, ck_ref, perm_ref, vt_ref, o_ref, kcat_ref, qcat_ref):
    tq, tk, hd, w = ATT_TQ, ATT_TK, FOX_HEAD_DIM, ATT_STRIP

    bias = jnp.dot(ck_ref[...], perm_ref[...], preferred_element_type=F32)
    keys = k_ref[...].astype(F32)
    lane_half = lax.broadcasted_iota(jnp.int32, (SEQ, LANES), 1) // hd
    queries = qt_ref[0].astype(F32)
    row = lax.broadcasted_iota(jnp.int32, (LANES, SEQ), 0)
    for hh in range(2):
        kcat_ref[hh] = jnp.where(lane_half == hh, keys, bias).astype(BF16)
        ones_at = (1 - hh) * hd
        ones_rows = ((row >= ones_at) & (row < ones_at + FOX_BIAS_TERMS)).astype(F32)
        qcat_ref[hh] = jnp.where(row // hd == hh, queries, ones_rows).astype(BF16)

    def q_body(qi, _):
        q0 = pl.multiple_of(qi * tq, tq)

        units = [(hh, st) for st in range(tq // w) for hh in range(2)]
        ones = jnp.ones((ATT_ONES_ROWS, tk), BF16)

        def scores(kj, diag, u):
            hh, st = units[u]
            k0 = pl.multiple_of(kj * tk, tk)
            c0 = pl.multiple_of(q0 + st * w, w)
            s = jnp.dot(kcat_ref[hh, pl.ds(k0, tk), :], qcat_ref[hh, :, pl.ds(c0, w)],
                        preferred_element_type=F32)
            if diag is not None and (diag + 1) * tk > st * w + 1:
                s_idx = lax.broadcasted_iota(jnp.int32, (tk, w), 0) + diag * tk
                t_idx = lax.broadcasted_iota(jnp.int32, (tk, w), 1) + st * w
                s = jnp.where(s_idx <= t_idx, s, -jnp.inf)
            return s

        def softmax(m, s):
            m_new = jnp.maximum(m, jnp.max(s, axis=0, keepdims=True))
            return m_new, jnp.exp2(m - m_new), jnp.exp2(s - m_new).astype(BF16)

        def weighted(kj, u, alpha, p, acc):
            hh, _ = units[u]
            k0 = pl.multiple_of(kj * tk, tk)
            vt = jnp.concatenate(
                [vt_ref[0, pl.ds(hh * hd, hd), pl.ds(k0, tk)], ones], axis=0)
            return alpha * acc + jnp.dot(vt, p, preferred_element_type=F32)

        def run(work, carry):
            la, lb = ATT_SCORE_LEAD, ATT_SCORE_LEAD + ATT_PV_LAG
            state, s_of, sm_of, n = list(carry), {}, {}, len(work)
            assert all(work[i][2] != work[j][2]
                       for i in range(n) for j in range(max(0, i - (lb - la)), i))
            for step in range(n + lb):
                if step < n:
                    s_of[step] = scores(*work[step])
                if la <= step < n + la:
                    u = work[step - la][2]
                    sm_of[step - la] = softmax(state[u][0], s_of.pop(step - la))
                if step >= lb:
                    kj, _, u = work[step - lb]
                    m_new, alpha, p = sm_of.pop(step - lb)
                    state[u] = (m_new, weighted(kj, u, alpha, p, state[u][1]))
            return tuple(state)

        init = tuple((jnp.full((1, w), -jnp.inf, F32),
                      jnp.zeros((hd + ATT_ONES_ROWS, w), F32)) for _ in units)
        n_full = qi * (tq // tk)

        def full_tiles(i, carry):
            return run([(i * ATT_KV_UNROLL + j, None, u)
                        for j in range(ATT_KV_UNROLL) for u in range(len(units))], carry)

        carry = lax.fori_loop(0, n_full // ATT_KV_UNROLL, full_tiles, init)
        carry = run([(n_full + d, d, u) for d in range(tq // tk)
                     for u in range(len(units)) if (units[u][1] + 1) * w > d * tk], carry)
        heads = []
        for hh in range(2):
            acc = jnp.concatenate([carry[u][1] for u in range(len(units))
                                   if units[u][0] == hh], axis=1)
            heads.append(acc[:hd] / acc[hd:hd + 1])
        o = jnp.concatenate(heads, axis=0)
        o_ref[pl.ds(q0, tq), :] = o.T.astype(BF16)
        return 0

    lax.fori_loop(0, SEQ // tq, q_body, 0)


def _fox_bias_permutation():
    pairs, hd = FOX_HEADS // 2, FOX_HEAD_DIM
    perm = np.zeros((pairs, LANES, LANES), np.float32)
    for p in range(pairs):
        for hh in range(2):
            for j in range(FOX_BIAS_TERMS):
                perm[p, FOX_GATE_LANES * (2 * p + hh) + j, (1 - hh) * hd + j] = 1.0
    return perm


def _fox_attn(qt, k, ck, vt):
    pairs = FOX_HEADS // 2
    perm = jnp.asarray(_fox_bias_permutation(), BF16)
    return pl.pallas_call(
        _fox_attn_kernel,
        grid=(BATCH, pairs),
        in_specs=[
            pl.BlockSpec((1, LANES, SEQ), lambda b, p: (b, p, 0)),
            pl.BlockSpec((SEQ, LANES), lambda b, p: (b, p)),
            pl.BlockSpec((SEQ, LANES), lambda b, p: (b, 0)),
            pl.BlockSpec((None, LANES, LANES), lambda b, p: (p, 0, 0)),
            pl.BlockSpec((1, LANES, SEQ), lambda b, p: (b, p, 0)),
        ],
        out_specs=pl.BlockSpec((SEQ, LANES), lambda b, p: (b, p)),
        out_shape=jax.ShapeDtypeStruct((TOKENS, D_MODEL), BF16),
        scratch_shapes=[pltpu.VMEM((2, SEQ, LANES), BF16),
                        pltpu.VMEM((2, LANES, SEQ), BF16)],
        compiler_params=_params("parallel", "parallel"),
        name="fox_attn",
    )(qt, k, ck, perm, vt)


def _gla_inproj_kernel(x_ref, g_ref, wq_ref, wk_ref, wv_ref, wr_ref, wz_ref,
                       wg_ref, bg_ref, q_ref, k_ref, v_ref, r_ref, gate_ref):
    hn = _rms(x_ref[...], g_ref[...]).astype(BF16)
    z = jnp.dot(hn, wz_ref[...], preferred_element_type=F32).astype(BF16)
    zg = jnp.dot(z, wg_ref[...], preferred_element_type=F32) + bg_ref[...]
    gate_ref[...] = _log_sigmoid(zg) * (LOG2E / GLA_GATE_TAU)
    q = jnp.dot(hn, wq_ref[...], preferred_element_type=F32)
    q_ref[...] = q * GLA_SCALE
    k_ref[...] = jnp.dot(hn, wk_ref[...], preferred_element_type=F32)
    v_ref[...] = jnp.dot(hn, wv_ref[...], preferred_element_type=F32).astype(BF16)
    r_ref[...] = jnp.dot(hn, wr_ref[...], preferred_element_type=F32)


def _gla_inproj(h, g, wq, wk, wv, wr, wz, wg, bg):
    tm = PROJ_TM
    rows = lambda i: (i, 0)
    return pl.pallas_call(
        _gla_inproj_kernel,
        grid=(TOKENS // tm,),
        in_specs=[
            pl.BlockSpec((tm, D_MODEL), rows),
            _resident((1, D_MODEL)),
            _resident((D_MODEL, GLA_KEY_DIM)),
            _resident((D_MODEL, GLA_KEY_DIM)),
            _resident((D_MODEL, D_MODEL)),
            _resident((D_MODEL, D_MODEL)),
            _resident((D_MODEL, LANES)),
            _resident((LANES, GLA_KEY_DIM)),
            _resident((1, GLA_KEY_DIM)),
        ],
        out_specs=[
            pl.BlockSpec((tm, GLA_KEY_DIM), rows),
            pl.BlockSpec((tm, GLA_KEY_DIM), rows),
            pl.BlockSpec((tm, D_MODEL), rows),
            pl.BlockSpec((tm, D_MODEL), rows),
            pl.BlockSpec((tm, GLA_KEY_DIM), rows),
        ],
        out_shape=[
            jax.ShapeDtypeStruct((TOKENS, GLA_KEY_DIM), F32),
            jax.ShapeDtypeStruct((TOKENS, GLA_KEY_DIM), F32),
            jax.ShapeDtypeStruct((TOKENS, D_MODEL), BF16),
            jax.ShapeDtypeStruct((TOKENS, D_MODEL), F32),
            jax.ShapeDtypeStruct((TOKENS, GLA_KEY_DIM), F32),
        ],
        compiler_params=_params("parallel"),
        name="gla_inproj",
    )(h, g, wq, wk, wv, wr, wz, wg, bg)


GLA_FINE_LEVELS = (1, 2, 4)
GLA_COARSE_LEVELS = (8, 16, 32)
GLA_LEVELS = GLA_FINE_LEVELS + GLA_COARSE_LEVELS
GLA_STEP_CHUNKS = 4
SUBLANES = 8
assert max(GLA_FINE_LEVELS) < SUBLANES <= min(GLA_COARSE_LEVELS)


def _gla_selection_matrix():
    c = CHUNK
    t = np.arange(c)[:, None]
    r = np.arange(c)[None, :]
    mats = [(r > m * (t // m)) & (r <= t) for m in GLA_FINE_LEVELS[1:]]
    for m in GLA_FINE_LEVELS:
        nxt = np.minimum(m * (t // m + 1), c - 1)
        mats.append((r > t) & (r <= nxt))
    mats.append(r <= t)
    sel = np.concatenate(mats, axis=0).astype(np.float32)
    return np.concatenate([sel, sel], axis=1)


GLA_SEL_ROWS = 2 * len(GLA_FINE_LEVELS) * CHUNK


def _gla_chunk_kernel(q_ref, k_ref, v_ref, r_ref, gate_ref, sel_ref, gn_ref,
                      y_ref, state_ref):
    c, nh, dk, dv = CHUNK, GLA_HEADS, GLA_DK, GLA_DV

    @pl.when(pl.program_id(1) == 0)
    def _():
        state_ref[...] = jnp.zeros_like(state_ref)

    t_idx = lax.broadcasted_iota(jnp.int32, (nh * c, LANES), 0) % c
    l_idx = lax.broadcasted_iota(jnp.int32, (nh * c, LANES), 1)
    h_idx = lax.broadcasted_iota(jnp.int32, (nh * c, LANES), 0) // c
    s_idx = l_idx % c
    x = t_idx ^ s_idx
    level = sum((x >= m).astype(jnp.int32) for m in GLA_LEVELS)
    level = jnp.where((l_idx // c == h_idx % (LANES // c)) & (s_idx <= t_idx), level, -1)

    def stack_heads(a):
        return jnp.concatenate([a[:, h * dk:(h + 1) * dk] for h in range(nh)], axis=0)

    def band(qq, kk):
        full = lax.dot_general(stack_heads(qq.astype(BF16)), stack_heads(kk.astype(BF16)),
                               NT_DIMS, preferred_element_type=F32)
        per_tile = LANES // c
        return jnp.concatenate(
            [full[h * c:(h + 1) * c, (h // per_tile) * LANES:(h // per_tile + 1) * LANES]
             for h in range(nh)], axis=0)

    def blockwise(b, m, side):
        out = []
        for j in range(c // m):
            blk = b[j * m:(j + 1) * m, :]
            if side == "q":
                out.append(blk - b[j * m:j * m + 1, :])
            else:
                nxt = min((j + 1) * m, c - 1)
                out.append(b[nxt:nxt + 1, :] - blk)
        return jnp.concatenate(out, axis=0)

    n_fine = len(GLA_FINE_LEVELS)

    def state_free_part(rows):
        q = q_ref[rows, :]
        k = k_ref[rows, :]
        gate = gate_ref[rows, :]
        g_hi = gate.astype(BF16)
        g_lo = (gate - g_hi.astype(F32)).astype(BF16)
        sums = jnp.dot(sel_ref[...], jnp.concatenate([g_hi, g_lo], axis=0),
                       preferred_element_type=F32)
        fine = jnp.exp2(sums[:(2 * n_fine - 1) * c, :])
        b = sums[(2 * n_fine - 1) * c:, :]
        attn = jnp.where(level == 0, band(q, k), 0.0)
        for li, m in enumerate(GLA_LEVELS):
            if m in GLA_FINE_LEVELS:
                qq = q if li == 0 else q * fine[(li - 1) * c:li * c, :]
                kk = k * fine[(n_fine - 1 + li) * c:(n_fine + li) * c, :]
            else:
                qq = q * jnp.exp2(blockwise(b, m, "q"))
                kk = k * jnp.exp2(blockwise(b, m, "k"))
            attn = jnp.where(level == li + 1, band(qq, kk), attn)
        e_q = jnp.exp2(b)
        qs = (q * e_q).astype(BF16)
        kd = (k * jnp.exp2(b[c - 1:c, :] - b)).astype(BF16)
        return attn.astype(BF16), qs, kd, e_q[c - 1:c, :]

    chunks = [pl.ds(ci * c, c) for ci in range(GLA_STEP_CHUNKS)]
    parts = [state_free_part(rows) for rows in chunks]
    gn = gn_ref[...]
    for rows, (attn, qs, kd, e_last) in zip(chunks, parts):
        for h in range(nh):
            v_h = v_ref[rows, h * dv:(h + 1) * dv]
            state_t = state_ref[h]
            o = jnp.dot(attn[h * c:(h + 1) * c, :],
                        jnp.concatenate([v_h] * (LANES // c), axis=0),
                        preferred_element_type=F32)
            o = o + lax.dot_general(qs[:, h * dk:(h + 1) * dk], state_t.astype(BF16),
                                    NT_DIMS, preferred_element_type=F32)
            state_ref[h] = state_t * e_last[:, h * dk:(h + 1) * dk] + lax.dot_general(
                v_h, kd[:, h * dk:(h + 1) * dk], TN_DIMS, preferred_element_type=F32)
            o = _rms(o, gn)
            r = r_ref[rows, h * dv:(h + 1) * dv]
            y_ref[rows, h * dv:(h + 1) * dv] = (o * (r * jax.nn.sigmoid(r))).astype(BF16)


def _gla_chunks(q, k, v, r, gate, sel, gn):
    tm = GLA_STEP_CHUNKS * CHUNK
    n = SEQ // tm
    rows = lambda b, j: (b * n + j, 0)
    full = lambda b, j: (0, 0)
    return pl.pallas_call(
        _gla_chunk_kernel,
        grid=(BATCH, n),
        in_specs=[
            pl.BlockSpec((tm, GLA_KEY_DIM), rows),
            pl.BlockSpec((tm, GLA_KEY_DIM), rows),
            pl.BlockSpec((tm, D_MODEL), rows),
            pl.BlockSpec((tm, D_MODEL), rows),
            pl.BlockSpec((tm, GLA_KEY_DIM), rows),
            pl.BlockSpec((GLA_SEL_ROWS, 2 * CHUNK), full),
            pl.BlockSpec((1, GLA_DV), full),
        ],
        out_specs=pl.BlockSpec((tm, D_MODEL), rows),
        out_shape=jax.ShapeDtypeStruct((TOKENS, D_MODEL), BF16),
        scratch_shapes=[pltpu.VMEM((GLA_HEADS, GLA_DV, GLA_DK), F32)],
        compiler_params=_params("parallel", "arbitrary"),
        name="gla_chunks",
    )(q, k, v, r, gate, sel, gn)


def _mlp_kernel(a_ref, wo_ref, res_ref, g_ref, wup_ref, wdn_ref, gf_ref, o_ref,
                *, final_norm):
    x = res_ref[...] + jnp.dot(a_ref[...], wo_ref[...], preferred_element_type=F32)
    hn = _rms(x, g_ref[...]).astype(BF16)
    for f in range(D_FF // MLP_TF):
        cols = slice(f * MLP_TF, (f + 1) * MLP_TF)
        u = jnp.maximum(jnp.dot(hn, wup_ref[:, cols], preferred_element_type=F32), 0.0)
        x = x + jnp.dot((u * u).astype(BF16), wdn_ref[cols, :],
                        preferred_element_type=F32)
    o_ref[...] = _rms(x, gf_ref[...]) if final_norm else x


def _mlp(a, w_out, jo, res, g, w_up, w_down, layer, gf, final_norm):
    tm = MLP_TM
    rows = lambda i: (i, 0)

    def layer_weights(shape, idx):
        return pl.BlockSpec((None,) + shape, lambda i: (idx, 0, 0),
                            pipeline_mode=pl.Buffered(1))

    return pl.pallas_call(
        functools.partial(_mlp_kernel, final_norm=final_norm),
        grid=(TOKENS // tm,),
        in_specs=[
            pl.BlockSpec((tm, D_MODEL), rows),
            layer_weights((D_MODEL, D_MODEL), jo),
            pl.BlockSpec((tm, D_MODEL), rows),
            _resident((1, D_MODEL)),
            layer_weights((D_MODEL, D_FF), layer),
            layer_weights((D_FF, D_MODEL), layer),
            _resident((1, D_MODEL)),
        ],
        out_specs=pl.BlockSpec((tm, D_MODEL), rows),
        out_shape=jax.ShapeDtypeStruct((TOKENS, D_MODEL), F32),
        compiler_params=_params("parallel"),
        name="mlp",
    )(a, w_out, res, g, w_up, w_down, gf)


def _row(v):
    return v.reshape(1, -1).astype(F32)


def _fox_mixer(h, g, w_in, b_f):
    d = D_MODEL
    wqt = w_in[:, :d].T.astype(BF16)
    wk = w_in[:, d:2 * d].astype(BF16)
    wvt = w_in[:, 2 * d:3 * d].T.astype(BF16)
    wf = jnp.repeat(w_in[:, 3 * d:], FOX_GATE_LANES, axis=1).astype(BF16)
    bf = _row(jnp.repeat(b_f, FOX_GATE_LANES))
    qt, k, vt, f = _fox_inproj(h, _row(g), wqt, wk, wvt, wf)
    return _fox_attn(qt, k, _fox_gate(f, bf), vt)


def _gla_mixer(h, g, w_in, w_gate_up, b_gate, g_norm, sel):
    kd, d = GLA_KEY_DIM, D_MODEL
    wq = w_in[:, :kd].astype(BF16)
    wk = w_in[:, kd:2 * kd].astype(BF16)
    wv = w_in[:, 2 * kd:2 * kd + d].astype(BF16)
    z0 = 2 * kd + d
    wz = jnp.pad(w_in[:, z0:z0 + GLA_GATE_RANK],
                 ((0, 0), (0, LANES - GLA_GATE_RANK))).astype(BF16)
    wr = w_in[:, z0 + GLA_GATE_RANK:].astype(BF16)
    wg = jnp.pad(w_gate_up, ((0, LANES - GLA_GATE_RANK), (0, 0))).astype(BF16)
    q, k, v, r, gate = _gla_inproj(h, _row(g), wq, wk, wv, wr, wz, wg, _row(b_gate))
    return _gla_chunks(q, k, v, r, gate, sel, _row(g_norm))


def kernel(x, fox_w_in, fox_b_f, fox_w_out, gla_w_in, gla_w_gate_up, gla_b_gate,
           gla_norm_g, gla_w_out, mlp_w_up, mlp_w_down, norm_mix_g, norm_mlp_g,
           norm_final_g):
    assert x.shape == (BATCH, SEQ, D_MODEL) and x.dtype == F32
    sel = jnp.asarray(_gla_selection_matrix(), BF16)
    w_up, w_down = mlp_w_up.astype(BF16), mlp_w_down.astype(BF16)
    w_out = (fox_w_out.astype(BF16), gla_w_out.astype(BF16))
    h = x.reshape(TOKENS, D_MODEL)
    for i in range(DEPTH):
        j = i // 2
        if i % 2 == 0:
            a = _fox_mixer(h, norm_mix_g[i], fox_w_in[j], fox_b_f[j])
        else:
            a = _gla_mixer(h, norm_mix_g[i], gla_w_in[j], gla_w_gate_up[j],
                           gla_b_gate[j], gla_norm_g[j], sel)
        h = _mlp(a, w_out[i % 2], j, h, _row(norm_mlp_g[i]), w_up, w_down, i,
                 _row(norm_final_g), final_norm=(i == DEPTH - 1))
    return h.reshape(BATCH, SEQ, D_MODEL)
```

```python
import functools

import numpy as np
import jax
import jax.numpy as jnp
from jax import lax
from jax.experimental import pallas as pl
from jax.experimental.pallas import tpu as pltpu

F32 = jnp.float32
BF16 = jnp.bfloat16

D_MODEL = 1024
BATCH = 4
SEQ = 4096
TOKENS = BATCH * SEQ
DEPTH = 4
EPS = 1e-6

FOX_HEADS = 16
FOX_HEAD_DIM = D_MODEL // FOX_HEADS
FOX_SCALE = FOX_HEAD_DIM ** -0.5
FOX_GATE_LANES = 8
FOX_BIAS_TERMS = 3
GLA_HEADS = 4
GLA_KEY_DIM = D_MODEL // 2
GLA_DK = GLA_KEY_DIM // GLA_HEADS
GLA_DV = D_MODEL // GLA_HEADS
GLA_GATE_RANK = 16
GLA_GATE_TAU = 16.0
GLA_SCALE = GLA_DK ** -0.5
CHUNK = 64
D_FF = 4 * D_MODEL

LANES = 128
VMEM_LIMIT = 48 * 1024 * 1024

PROJ_TM = 1024
MLP_TM = 512
MLP_TF = 1024
ATT_TQ = 2048
ATT_TK = 512
ATT_KV_UNROLL = 2
ATT_SCORE_LEAD = 2
ATT_PV_LAG = 1
assert ATT_TQ % (ATT_TK * ATT_KV_UNROLL) == 0
ATT_STRIP = 512
ATT_ONES_ROWS = 16
LOG2E = 1.4426950408889634
GATE_BLOCK = 256

NT_DIMS = (((1,), (1,)), ((), ()))
TN_DIMS = (((0,), (0,)), ((), ()))


def _params(*semantics):
    return pltpu.CompilerParams(dimension_semantics=semantics,
                                vmem_limit_bytes=VMEM_LIMIT)


def _resident(shape):
    return pl.BlockSpec(shape, lambda *_: (0,) * len(shape),
                        pipeline_mode=pl.Buffered(1))


def _rms(x, g):
    ms = jnp.mean(x * x, axis=-1, keepdims=True)
    return x * lax.rsqrt(ms + EPS) * g


def _log_sigmoid(x):
    return jnp.minimum(x, 0.0) - jnp.log1p(jnp.exp(-jnp.abs(x)))


def _split3(x):
    hi = x.astype(BF16)
    r1 = x - hi.astype(F32)
    mid = r1.astype(BF16)
    lo = (r1 - mid.astype(F32)).astype(BF16)
    return hi, mid, lo


def _fox_inproj_kernel(x_ref, g_ref, wqt_ref, wk_ref, wvt_ref, wf_ref,
                       qt_ref, k_ref, vt_ref, f_ref):
    hn = _rms(x_ref[...], g_ref[...]).astype(BF16)
    qt = lax.dot_general(wqt_ref[...], hn, NT_DIMS, preferred_element_type=F32)
    qt_ref[0] = (qt * (FOX_SCALE * LOG2E)).astype(BF16)
    k_ref[...] = jnp.dot(hn, wk_ref[...], preferred_element_type=F32).astype(BF16)
    vt = lax.dot_general(wvt_ref[...], hn, NT_DIMS, preferred_element_type=F32)
    vt_ref[0] = vt.astype(BF16)
    f_ref[...] = jnp.dot(hn, wf_ref[...], preferred_element_type=F32)


def _fox_inproj(h, g, wqt, wk, wvt, wf):
    tm = PROJ_TM
    per_b = SEQ // tm
    return pl.pallas_call(
        _fox_inproj_kernel,
        grid=(TOKENS // tm,),
        in_specs=[
            pl.BlockSpec((tm, D_MODEL), lambda i: (i, 0)),
            _resident((1, D_MODEL)),
            _resident((D_MODEL, D_MODEL)),
            _resident((D_MODEL, D_MODEL)),
            _resident((D_MODEL, D_MODEL)),
            _resident((D_MODEL, LANES)),
        ],
        out_specs=[
            pl.BlockSpec((1, D_MODEL, tm), lambda i: (i // per_b, 0, i % per_b)),
            pl.BlockSpec((tm, D_MODEL), lambda i: (i, 0)),
            pl.BlockSpec((1, D_MODEL, tm), lambda i: (i // per_b, 0, i % per_b)),
            pl.BlockSpec((tm, LANES), lambda i: (i, 0)),
        ],
        out_shape=[
            jax.ShapeDtypeStruct((BATCH, D_MODEL, SEQ), BF16),
            jax.ShapeDtypeStruct((TOKENS, D_MODEL), BF16),
            jax.ShapeDtypeStruct((BATCH, D_MODEL, SEQ), BF16),
            jax.ShapeDtypeStruct((TOKENS, LANES), F32),
        ],
        compiler_params=_params("parallel"),
        name="fox_inproj",
    )(h, g, wqt, wk, wvt, wf)


def _fox_gate_kernel(f_ref, bf_ref, ck_ref):
    bl = GATE_BLOCK
    row = lax.broadcasted_iota(jnp.int32, (bl, bl), 0)
    col = lax.broadcasted_iota(jnp.int32, (bl, bl), 1)
    tril = (col <= row).astype(BF16)
    j = lax.broadcasted_iota(jnp.int32, (1, LANES), 1) % FOX_GATE_LANES

    def body(i, carry):
        r0 = pl.multiple_of(i * bl, bl)
        lf = _log_sigmoid(f_ref[pl.ds(r0, bl), :] + bf_ref[...])
        hi, mid, lo = _split3(lf)
        c = carry + (jnp.dot(tril, hi, preferred_element_type=F32)
                     + jnp.dot(tril, mid, preferred_element_type=F32)
                     + jnp.dot(tril, lo, preferred_element_type=F32))
        chi, cmid, clo = (t.astype(F32) for t in _split3(c * LOG2E))
        ck = jnp.where(j == 0, -chi, jnp.where(j == 1, -cmid, jnp.where(
            j == 2, -clo, 0.0)))
        ck_ref[pl.ds(r0, bl), :] = ck.astype(BF16)
        return c[bl - 1:bl, :]

    lax.fori_loop(0, SEQ // bl, body, jnp.zeros((1, LANES), F32))


def _fox_gate(f, bf):
    return pl.pallas_call(
        _fox_gate_kernel,
        grid=(BATCH,),
        in_specs=[pl.BlockSpec((SEQ, LANES), lambda b: (b, 0)),
                  pl.BlockSpec((1, LANES), lambda b: (0, 0))],
        out_specs=pl.BlockSpec((SEQ, LANES), lambda b: (b, 0)),
        out_shape=jax.ShapeDtypeStruct((TOKENS, LANES), BF16),
        compiler_params=_params("parallel"),
        name="fox_gate",
    )(f, bf)


def _fox_attn_kernel(qt_ref, k_ref, ck_ref, perm_ref, vt_ref, o_ref, kcat_ref, qcat_ref):
    tq, tk, hd, w = ATT_TQ, ATT_TK, FOX_HEAD_DIM, ATT_STRIP

    bias = jnp.dot(ck_ref[...], perm_ref[...], preferred_element_type=F32)
    keys = k_ref[...].astype(F32)
    lane_half = lax.broadcasted_iota(jnp.int32, (SEQ, LANES), 1) // hd
    queries = qt_ref[0].astype(F32)
    row = lax.broadcasted_iota(jnp.int32, (LANES, SEQ), 0)
    for hh in range(2):
        kcat_ref[hh] = jnp.where(lane_half == hh, keys, bias).astype(BF16)
        ones_at = (1 - hh) * hd
        ones_rows = ((row >= ones_at) & (row < ones_at + FOX_BIAS_TERMS)).astype(F32)
        qcat_ref[hh] = jnp.where(row // hd == hh, queries, ones_rows).astype(BF16)

    strips, tiles_per_q = tq // w, tq // tk
    units = [(qi, hh, st) for qi in range(SEQ // tq) for st in range(strips)
             for hh in range(2)]
    of_tile = lambda qi: [u for u in range(len(units)) if units[u][0] == qi]
    ones = jnp.ones((ATT_ONES_ROWS, tk), BF16)

    def scores(kj, diag, u):
        qi, hh, st = units[u]
        k0 = pl.multiple_of(kj * tk, tk)
        s = jnp.dot(kcat_ref[hh, pl.ds(k0, tk), :],
                    qcat_ref[hh, :, pl.ds(qi * tq + st * w, w)],
                    preferred_element_type=F32)
        if diag is not None and (diag + 1) * tk > st * w + 1:
            s_idx = lax.broadcasted_iota(jnp.int32, (tk, w), 0) + diag * tk
            t_idx = lax.broadcasted_iota(jnp.int32, (tk, w), 1) + st * w
            s = jnp.where(s_idx <= t_idx, s, -jnp.inf)
        return s

    def softmax(m, s):
        m_new = jnp.maximum(m, jnp.max(s, axis=0, keepdims=True))
        return m_new, jnp.exp2(m - m_new), jnp.exp2(s - m_new).astype(BF16)

    def weighted(kj, u, alpha, p, acc):
        hh = units[u][1]
        k0 = pl.multiple_of(kj * tk, tk)
        vt = jnp.concatenate(
            [vt_ref[0, pl.ds(hh * hd, hd), pl.ds(k0, tk)], ones], axis=0)
        return alpha * acc + jnp.dot(vt, p, preferred_element_type=F32)

    def run(work, state):
        la, lb = ATT_SCORE_LEAD, ATT_SCORE_LEAD + ATT_PV_LAG
        state, s_of, sm_of, n = dict(state), {}, {}, len(work)
        assert all(work[i][2] != work[j][2]
                   for i in range(n) for j in range(max(0, i - (lb - la)), i))
        for step in range(n + lb):
            if step < n:
                s_of[step] = scores(*work[step])
            if la <= step < n + la:
                u = work[step - la][2]
                sm_of[step - la] = softmax(state[u][0], s_of.pop(step - la))
            if step >= lb:
                kj, _, u = work[step - lb]
                m_new, alpha, p = sm_of.pop(step - lb)
                state[u] = (m_new, weighted(kj, u, alpha, p, state[u][1]))
        return state

    state = {u: (jnp.full((1, w), -jnp.inf, F32),
                 jnp.zeros((hd + ATT_ONES_ROWS, w), F32)) for u in range(len(units))}

    for qi in range(1, SEQ // tq):
        mine = of_tile(qi)

        def full_tiles(i, carry, mine=mine):
            out = run([(i * ATT_KV_UNROLL + j, None, u)
                       for j in range(ATT_KV_UNROLL) for u in mine], dict(zip(mine, carry)))
            return tuple(out[u] for u in mine)

        carry = lax.fori_loop(0, qi * tiles_per_q // ATT_KV_UNROLL, full_tiles,
                              tuple(state[u] for u in mine))
        state.update(zip(mine, carry))

    per_tile = [[(qi * tiles_per_q + d, d, u) for d in range(tiles_per_q)
                 for u in of_tile(qi) if (units[u][2] + 1) * w > d * tk]
                for qi in range(SEQ // tq)]
    state = run([item for group in zip(*per_tile) for item in group], state)

    for qi in range(SEQ // tq):
        heads = []
        for hh in range(2):
            acc = jnp.concatenate([state[u][1] for u in of_tile(qi)
                                   if units[u][1] == hh], axis=1)
            heads.append(acc[:hd] / acc[hd:hd + 1])
        o = jnp.concatenate(heads, axis=0)
        o_ref[qi * tq:(qi + 1) * tq, :] = o.T.astype(BF16)


def _fox_bias_permutation():
    pairs, hd = FOX_HEADS // 2, FOX_HEAD_DIM
    perm = np.zeros((pairs, LANES, LANES), np.float32)
    for p in range(pairs):
        for hh in range(2):
            for j in range(FOX_BIAS_TERMS):
                perm[p, FOX_GATE_LANES * (2 * p + hh) + j, (1 - hh) * hd + j] = 1.0
    return perm


def _fox_attn(qt, k, ck, vt):
    pairs = FOX_HEADS // 2
    perm = jnp.asarray(_fox_bias_permutation(), BF16)
    return pl.pallas_call(
        _fox_attn_kernel,
        grid=(BATCH, pairs),
        in_specs=[
            pl.BlockSpec((1, LANES, SEQ), lambda b, p: (b, p, 0)),
            pl.BlockSpec((SEQ, LANES), lambda b, p: (b, p)),
            pl.BlockSpec((SEQ, LANES), lambda b, p: (b, 0)),
            pl.BlockSpec((None, LANES, LANES), lambda b, p: (p, 0, 0)),
            pl.BlockSpec((1, LANES, SEQ), lambda b, p: (b, p, 0)),
        ],
        out_specs=pl.BlockSpec((SEQ, LANES), lambda b, p: (b, p)),
        out_shape=jax.ShapeDtypeStruct((TOKENS, D_MODEL), BF16),
        scratch_shapes=[pltpu.VMEM((2, SEQ, LANES), BF16),
                        pltpu.VMEM((2, LANES, SEQ), BF16)],
        compiler_params=_params("parallel", "parallel"),
        name="fox_attn",
    )(qt, k, ck, perm, vt)


def _gla_inproj_kernel(x_ref, g_ref, wq_ref, wk_ref, wv_ref, wr_ref, wzt_ref,
                       wg_ref, bg_ref, q_ref, k_ref, v_ref, r_ref, gate_ref):
    hn = _rms(x_ref[...], g_ref[...]).astype(BF16)
    zt = lax.dot_general(wzt_ref[...], hn, NT_DIMS,
                         preferred_element_type=F32).astype(BF16)
    zg = lax.dot_general(zt, wg_ref[...], TN_DIMS,
                         preferred_element_type=F32) + bg_ref[...]
    gate_ref[...] = _log_sigmoid(zg) * (LOG2E / GLA_GATE_TAU)
    q = jnp.dot(hn, wq_ref[...], preferred_element_type=F32)
    q_ref[...] = q * GLA_SCALE
    k_ref[...] = jnp.dot(hn, wk_ref[...], preferred_element_type=F32)
    v_ref[...] = jnp.dot(hn, wv_ref[...], preferred_element_type=F32).astype(BF16)
    r_ref[...] = jnp.dot(hn, wr_ref[...], preferred_element_type=F32)


def _gla_inproj(h, g, wq, wk, wv, wr, wzt, wg, bg):
    tm = PROJ_TM
    rows = lambda i: (i, 0)
    return pl.pallas_call(
        _gla_inproj_kernel,
        grid=(TOKENS // tm,),
        in_specs=[
            pl.BlockSpec((tm, D_MODEL), rows),
            _resident((1, D_MODEL)),
            _resident((D_MODEL, GLA_KEY_DIM)),
            _resident((D_MODEL, GLA_KEY_DIM)),
            _resident((D_MODEL, D_MODEL)),
            _resident((D_MODEL, D_MODEL)),
            _resident((GLA_GATE_RANK, D_MODEL)),
            _resident((GLA_GATE_RANK, GLA_KEY_DIM)),
            _resident((1, GLA_KEY_DIM)),
        ],
        out_specs=[
            pl.BlockSpec((tm, GLA_KEY_DIM), rows),
            pl.BlockSpec((tm, GLA_KEY_DIM), rows),
            pl.BlockSpec((tm, D_MODEL), rows),
            pl.BlockSpec((tm, D_MODEL), rows),
            pl.BlockSpec((tm, GLA_KEY_DIM), rows),
        ],
        out_shape=[
            jax.ShapeDtypeStruct((TOKENS, GLA_KEY_DIM), F32),
            jax.ShapeDtypeStruct((TOKENS, GLA_KEY_DIM), F32),
            jax.ShapeDtypeStruct((TOKENS, D_MODEL), BF16),
            jax.ShapeDtypeStruct((TOKENS, D_MODEL), F32),
            jax.ShapeDtypeStruct((TOKENS, GLA_KEY_DIM), F32),
        ],
        compiler_params=_params("parallel"),
        name="gla_inproj",
    )(h, g, wq, wk, wv, wr, wzt, wg, bg)


GLA_FINE_LEVELS = (1, 2, 4)
GLA_COARSE_LEVELS = (8, 16, 32)
GLA_LEVELS = GLA_FINE_LEVELS + GLA_COARSE_LEVELS
GLA_STEP_CHUNKS = 8
SUBLANES = 8
assert max(GLA_FINE_LEVELS) < SUBLANES <= min(GLA_COARSE_LEVELS)


def _gla_selection_matrix():
    c = CHUNK
    t = np.arange(c)[:, None]
    r = np.arange(c)[None, :]
    mats = [(r > m * (t // m)) & (r <= t) for m in GLA_FINE_LEVELS[1:]]
    for m in GLA_FINE_LEVELS:
        nxt = np.minimum(m * (t // m + 1), c - 1)
        mats.append((r > t) & (r <= nxt))
    mats.append(r <= t)
    sel = np.concatenate(mats, axis=0).astype(np.float32)
    return np.concatenate([sel, sel], axis=1)


GLA_SEL_ROWS = 2 * len(GLA_FINE_LEVELS) * CHUNK


def _gla_chunk_kernel(q_ref, k_ref, v_ref, r_ref, gate_ref, sel_ref, gn_ref,
                      y_ref, state_ref):
    c, nh, dk, dv = CHUNK, GLA_HEADS, GLA_DK, GLA_DV

    @pl.when(pl.program_id(1) == 0)
    def _():
        state_ref[...] = jnp.zeros_like(state_ref)

    t_idx = lax.broadcasted_iota(jnp.int32, (nh * c, LANES), 0) % c
    l_idx = lax.broadcasted_iota(jnp.int32, (nh * c, LANES), 1)
    h_idx = lax.broadcasted_iota(jnp.int32, (nh * c, LANES), 0) // c
    s_idx = l_idx % c
    x = t_idx ^ s_idx
    level = sum((x >= m).astype(jnp.int32) for m in GLA_LEVELS)
    level = jnp.where((l_idx // c == h_idx % (LANES // c)) & (s_idx <= t_idx), level, -1)

    def stack_heads(a):
        return jnp.concatenate([a[:, h * dk:(h + 1) * dk] for h in range(nh)], axis=0)

    def band(qq, kk):
        full = lax.dot_general(stack_heads(qq.astype(BF16)), stack_heads(kk.astype(BF16)),
                               NT_DIMS, preferred_element_type=F32)
        per_tile = LANES // c
        return jnp.concatenate(
            [full[h * c:(h + 1) * c, (h // per_tile) * LANES:(h // per_tile + 1) * LANES]
             for h in range(nh)], axis=0)

    def blockwise(b, m, side):
        out = []
        for j in range(c // m):
            blk = b[j * m:(j + 1) * m, :]
            if side == "q":
                out.append(blk - b[j * m:j * m + 1, :])
            else:
                nxt = min((j + 1) * m, c - 1)
                out.append(b[nxt:nxt + 1, :] - blk)
        return jnp.concatenate(out, axis=0)

    n_fine = len(GLA_FINE_LEVELS)

    def state_free_part(rows):
        q = q_ref[rows, :]
        k = k_ref[rows, :]
        gate = gate_ref[rows, :]
        g_hi = gate.astype(BF16)
        g_lo = (gate - g_hi.astype(F32)).astype(BF16)
        sums = jnp.dot(sel_ref[...], jnp.concatenate([g_hi, g_lo], axis=0),
                       preferred_element_type=F32)
        fine = jnp.exp2(sums[:(2 * n_fine - 1) * c, :])
        b = sums[(2 * n_fine - 1) * c:, :]
        attn = jnp.where(level == 0, band(q, k), 0.0)
        for li, m in enumerate(GLA_LEVELS):
            if m in GLA_FINE_LEVELS:
                qq = q if li == 0 else q * fine[(li - 1) * c:li * c, :]
                kk = k * fine[(n_fine - 1 + li) * c:(n_fine + li) * c, :]
            else:
                qq = q * jnp.exp2(blockwise(b, m, "q"))
                kk = k * jnp.exp2(blockwise(b, m, "k"))
            attn = jnp.where(level == li + 1, band(qq, kk), attn)
        e_q = jnp.exp2(b)
        qs = (q * e_q).astype(BF16)
        kd = (k * jnp.exp2(b[c - 1:c, :] - b)).astype(BF16)
        return attn.astype(BF16), qs, kd, e_q[c - 1:c, :]

    chunks = [pl.ds(ci * c, c) for ci in range(GLA_STEP_CHUNKS)]
    parts = [state_free_part(rows) for rows in chunks]
    gn = gn_ref[...]
    for rows, (attn, qs, kd, e_last) in zip(chunks, parts):
        for h in range(nh):
            v_h = v_ref[rows, h * dv:(h + 1) * dv]
            state_t = state_ref[h]
            o = jnp.dot(attn[h * c:(h + 1) * c, :],
                        jnp.concatenate([v_h] * (LANES // c), axis=0),
                        preferred_element_type=F32)
            o = o + lax.dot_general(qs[:, h * dk:(h + 1) * dk], state_t.astype(BF16),
                                    NT_DIMS, preferred_element_type=F32)
            state_ref[h] = state_t * e_last[:, h * dk:(h + 1) * dk] + lax.dot_general(
                v_h, kd[:, h * dk:(h + 1) * dk], TN_DIMS, preferred_element_type=F32)
            o = _rms(o, gn)
            r = r_ref[rows, h * dv:(h + 1) * dv]
            y_ref[rows, h * dv:(h + 1) * dv] = (o * (r * jax.nn.sigmoid(r))).astype(BF16)


def _gla_chunks(q, k, v, r, gate, sel, gn):
    tm = GLA_STEP_CHUNKS * CHUNK
    n = SEQ // tm
    rows = lambda b, j: (b * n + j, 0)
    full = lambda b, j: (0, 0)
    return pl.pallas_call(
        _gla_chunk_kernel,
        grid=(BATCH, n),
        in_specs=[
            pl.BlockSpec((tm, GLA_KEY_DIM), rows),
            pl.BlockSpec((tm, GLA_KEY_DIM), rows),
            pl.BlockSpec((tm, D_MODEL), rows),
            pl.BlockSpec((tm, D_MODEL), rows),
            pl.BlockSpec((tm, GLA_KEY_DIM), rows),
            pl.BlockSpec((GLA_SEL_ROWS, 2 * CHUNK), full),
            pl.BlockSpec((1, GLA_DV), full),
        ],
        out_specs=pl.BlockSpec((tm, D_MODEL), rows),
        out_shape=jax.ShapeDtypeStruct((TOKENS, D_MODEL), BF16),
        scratch_shapes=[pltpu.VMEM((GLA_HEADS, GLA_DV, GLA_DK), F32)],
        compiler_params=_params("parallel", "arbitrary"),
        name="gla_chunks",
    )(q, k, v, r, gate, sel, gn)


def _mlp_kernel(a_ref, wo_ref, res_ref, g_ref, wup_ref, wdn_ref, gf_ref, o_ref,
                *, final_norm):
    x = res_ref[...] + jnp.dot(a_ref[...], wo_ref[...], preferred_element_type=F32)
    hn = _rms(x, g_ref[...]).astype(BF16)
    for f in range(D_FF // MLP_TF):
        cols = slice(f * MLP_TF, (f + 1) * MLP_TF)
        u = jnp.maximum(jnp.dot(hn, wup_ref[:, cols], preferred_element_type=F32), 0.0)
        x = x + jnp.dot((u * u).astype(BF16), wdn_ref[cols, :],
                        preferred_element_type=F32)
    o_ref[...] = _rms(x, gf_ref[...]) if final_norm else x


def _mlp(a, w_out, jo, res, g, w_up, w_down, layer, gf, final_norm):
    tm = MLP_TM
    rows = lambda i: (i, 0)

    def layer_weights(shape, idx):
        return pl.BlockSpec((None,) + shape, lambda i: (idx, 0, 0),
                            pipeline_mode=pl.Buffered(1))

    return pl.pallas_call(
        functools.partial(_mlp_kernel, final_norm=final_norm),
        grid=(TOKENS // tm,),
        in_specs=[
            pl.BlockSpec((tm, D_MODEL), rows),
            layer_weights((D_MODEL, D_MODEL), jo),
            pl.BlockSpec((tm, D_MODEL), rows),
            _resident((1, D_MODEL)),
            layer_weights((D_MODEL, D_FF), layer),
            layer_weights((D_FF, D_MODEL), layer),
            _resident((1, D_MODEL)),
        ],
        out_specs=pl.BlockSpec((tm, D_MODEL), rows),
        out_shape=jax.ShapeDtypeStruct((TOKENS, D_MODEL), F32),
        compiler_params=_params("parallel"),
        name="mlp",
    )(a, w_out, res, g, w_up, w_down, gf)


def _row(v):
    return v.reshape(1, -1).astype(F32)


def _fox_mixer(h, g, w_in, b_f):
    d = D_MODEL
    wqt = w_in[:, :d].T.astype(BF16)
    wk = w_in[:, d:2 * d].astype(BF16)
    wvt = w_in[:, 2 * d:3 * d].T.astype(BF16)
    wf = jnp.repeat(w_in[:, 3 * d:], FOX_GATE_LANES, axis=1).astype(BF16)
    bf = _row(jnp.repeat(b_f, FOX_GATE_LANES))
    qt, k, vt, f = _fox_inproj(h, _row(g), wqt, wk, wvt, wf)
    return _fox_attn(qt, k, _fox_gate(f, bf), vt)


def _gla_mixer(h, g, w_in, w_gate_up, b_gate, g_norm, sel):
    kd, d = GLA_KEY_DIM, D_MODEL
    wq = w_in[:, :kd].astype(BF16)
    wk = w_in[:, kd:2 * kd].astype(BF16)
    wv = w_in[:, 2 * kd:2 * kd + d].astype(BF16)
    z0 = 2 * kd + d
    wzt = w_in[:, z0:z0 + GLA_GATE_RANK].T.astype(BF16)
    wr = w_in[:, z0 + GLA_GATE_RANK:].astype(BF16)
    wg = w_gate_up.astype(BF16)
    q, k, v, r, gate = _gla_inproj(h, _row(g), wq, wk, wv, wr, wzt, wg, _row(b_gate))
    return _gla_chunks(q, k, v, r, gate, sel, _row(g_norm))


def kernel(x, fox_w_in, fox_b_f, fox_w_out, gla_w_in, gla_w_gate_up, gla_b_gate,
           gla_norm_g, gla_w_out, mlp_w_up, mlp_w_down, norm_mix_g, norm_mlp_g,
           norm_final_g):
    assert x.shape == (BATCH, SEQ, D_MODEL) and x.dtype == F32
    sel = jnp.asarray(_gla_selection_matrix(), BF16)
    w_up, w_down = mlp_w_up.astype(BF16), mlp_w_down.astype(BF16)
    w_out = (fox_w_out.astype(BF16), gla_w_out.astype(BF16))
    h = x.reshape(TOKENS, D_MODEL)
    for i in range(DEPTH):
        j = i // 2
        if i % 2 == 0:
            a = _fox_mixer(h, norm_mix_g[i], fox_w_in[j], fox_b_f[j])
        else:
            a = _gla_mixer(h, norm_mix_g[i], gla_w_in[j], gla_w_gate_up[j],
                           gla_b_gate[j], gla_norm_g[j], sel)
        h = _mlp(a, w_out[i % 2], j, h, _row(norm_mlp_g[i]), w_up, w_down, i,
                 _row(norm_final_g), final_norm=(i == DEPTH - 1))
    return h.reshape(BATCH, SEQ, D_MODEL)
```

```python
import functools

import numpy as np
import jax
import jax.numpy as jnp
from jax import lax
from jax.experimental import pallas as pl
from jax.experimental.pallas import tpu as pltpu

F32 = jnp.float32
BF16 = jnp.bfloat16

D_MODEL = 1024
BATCH = 4
SEQ = 4096
TOKENS = BATCH * SEQ
DEPTH = 4
EPS = 1e-6

FOX_HEADS = 16
FOX_HEAD_DIM = D_MODEL // FOX_HEADS
FOX_SCALE = FOX_HEAD_DIM ** -0.5
FOX_GATE_LANES = 8
FOX_BIAS_TERMS = 3
GLA_HEADS = 4
GLA_KEY_DIM = D_MODEL // 2
GLA_DK = GLA_KEY_DIM // GLA_HEADS
GLA_DV = D_MODEL // GLA_HEADS
GLA_GATE_RANK = 16
GLA_GATE_TAU = 16.0
GLA_SCALE = GLA_DK ** -0.5
CHUNK = 64
D_FF = 4 * D_MODEL

LANES = 128
VMEM_LIMIT = 56 * 1024 * 1024

PROJ_TM = 1024
MLP_TM = 512
MLP_TF = 1024
ATT_TQ = 2048
ATT_TK = 512
ATT_KV_UNROLL = 2
ATT_SCORE_LEAD = 2
ATT_PV_LAG = 1
assert ATT_TQ % (ATT_TK * ATT_KV_UNROLL) == 0
ATT_STRIP = 512
ATT_ONES_ROWS = 16
LOG2E = 1.4426950408889634
GATE_BLOCK = 256

NT_DIMS = (((1,), (1,)), ((), ()))
TN_DIMS = (((0,), (0,)), ((), ()))


def _params(*semantics):
    return pltpu.CompilerParams(dimension_semantics=semantics,
                                vmem_limit_bytes=VMEM_LIMIT)


def _resident(shape):
    return pl.BlockSpec(shape, lambda *_: (0,) * len(shape),
                        pipeline_mode=pl.Buffered(1))


def _rms(x, g):
    ms = jnp.mean(x * x, axis=-1, keepdims=True)
    return x * lax.rsqrt(ms + EPS) * g


def _log_sigmoid(x):
    return jnp.minimum(x, 0.0) - jnp.log1p(jnp.exp(-jnp.abs(x)))


def _split3(x):
    hi = x.astype(BF16)
    r1 = x - hi.astype(F32)
    mid = r1.astype(BF16)
    lo = (r1 - mid.astype(F32)).astype(BF16)
    return hi, mid, lo


def _fox_inproj_kernel(x_ref, g_ref, wqt_ref, wk_ref, wvt_ref, wf_ref, bf_ref,
                       qt_ref, k_ref, vt_ref, ck_ref, carry_ref):
    bl = GATE_BLOCK

    @pl.when(pl.program_id(0) % (SEQ // PROJ_TM) == 0)
    def _():
        carry_ref[...] = jnp.zeros_like(carry_ref)

    hn = _rms(x_ref[...], g_ref[...]).astype(BF16)

    qt = lax.dot_general(wqt_ref[...], hn, NT_DIMS, preferred_element_type=F32)
    qt_ref[0] = (qt * (FOX_SCALE * LOG2E)).astype(BF16)

    lf = _log_sigmoid(jnp.dot(hn, wf_ref[...], preferred_element_type=F32) + bf_ref[...])
    row = lax.broadcasted_iota(jnp.int32, (bl, bl), 0)
    col = lax.broadcasted_iota(jnp.int32, (bl, bl), 1)
    tril = (col <= row).astype(BF16)
    j = lax.broadcasted_iota(jnp.int32, (1, LANES), 1) % FOX_GATE_LANES
    carry = carry_ref[...]
    for i in range(PROJ_TM // bl):
        hi, mid, lo = _split3(lf[i * bl:(i + 1) * bl, :])
        c = carry + (jnp.dot(tril, hi, preferred_element_type=F32)
                     + jnp.dot(tril, mid, preferred_element_type=F32)
                     + jnp.dot(tril, lo, preferred_element_type=F32))
        chi, cmid, clo = (t.astype(F32) for t in _split3(c * LOG2E))
        ck = jnp.where(j == 0, -chi, jnp.where(j == 1, -cmid, jnp.where(
            j == 2, -clo, 0.0)))
        ck_ref[i * bl:(i + 1) * bl, :] = ck.astype(BF16)
        carry = c[bl - 1:bl, :]
    carry_ref[...] = carry

    k_ref[...] = jnp.dot(hn, wk_ref[...], preferred_element_type=F32).astype(BF16)
    vt = lax.dot_general(wvt_ref[...], hn, NT_DIMS, preferred_element_type=F32)
    vt_ref[0] = vt.astype(BF16)


def _fox_inproj(h, g, wqt, wk, wvt, wf, bf):
    tm = PROJ_TM
    per_b = SEQ // tm
    return pl.pallas_call(
        _fox_inproj_kernel,
        grid=(TOKENS // tm,),
        in_specs=[
            pl.BlockSpec((tm, D_MODEL), lambda i: (i, 0)),
            _resident((1, D_MODEL)),
            _resident((D_MODEL, D_MODEL)),
            _resident((D_MODEL, D_MODEL)),
            _resident((D_MODEL, D_MODEL)),
            _resident((D_MODEL, LANES)),
            _resident((1, LANES)),
        ],
        out_specs=[
            pl.BlockSpec((1, D_MODEL, tm), lambda i: (i // per_b, 0, i % per_b)),
            pl.BlockSpec((tm, D_MODEL), lambda i: (i, 0)),
            pl.BlockSpec((1, D_MODEL, tm), lambda i: (i // per_b, 0, i % per_b)),
            pl.BlockSpec((tm, LANES), lambda i: (i, 0)),
        ],
        out_shape=[
            jax.ShapeDtypeStruct((BATCH, D_MODEL, SEQ), BF16),
            jax.ShapeDtypeStruct((TOKENS, D_MODEL), BF16),
            jax.ShapeDtypeStruct((BATCH, D_MODEL, SEQ), BF16),
            jax.ShapeDtypeStruct((TOKENS, LANES), BF16),
        ],
        scratch_shapes=[pltpu.VMEM((1, LANES), F32)],
        compiler_params=_params("arbitrary"),
        name="fox_inproj",
    )(h, g, wqt, wk, wvt, wf, bf)


def _fox_attn_kernel(qt_ref, k_ref, ck_ref, perm_ref, vt_ref, o_ref, kcat_ref, qcat_ref):
    tq, tk, hd, w = ATT_TQ, ATT_TK, FOX_HEAD_DIM, ATT_STRIP

    bias = jnp.dot(ck_ref[...], perm_ref[...], preferred_element_type=F32)
    keys = k_ref[...].astype(F32)
    lane_half = lax.broadcasted_iota(jnp.int32, (SEQ, LANES), 1) // hd
    queries = qt_ref[0].astype(F32)
    row = lax.broadcasted_iota(jnp.int32, (LANES, SEQ), 0)
    for hh in range(2):
        kcat_ref[hh] = jnp.where(lane_half == hh, keys, bias).astype(BF16)
        ones_at = (1 - hh) * hd
        ones_rows = ((row >= ones_at) & (row < ones_at + FOX_BIAS_TERMS)).astype(F32)
        qcat_ref[hh] = jnp.where(row // hd == hh, queries, ones_rows).astype(BF16)

    strips, tiles_per_q = tq // w, tq // tk
    units = [(qi, hh, st) for qi in range(SEQ // tq) for st in range(strips)
             for hh in range(2)]
    of_tile = lambda qi: [u for u in range(len(units)) if units[u][0] == qi]
    ones = jnp.ones((ATT_ONES_ROWS, tk), BF16)

    def scores(kj, diag, u):
        qi, hh, st = units[u]
        k0 = pl.multiple_of(kj * tk, tk)
        s = jnp.dot(kcat_ref[hh, pl.ds(k0, tk), :],
                    qcat_ref[hh, :, pl.ds(qi * tq + st * w, w)],
                    preferred_element_type=F32)
        if diag is not None and (diag + 1) * tk > st * w + 1:
            s_idx = lax.broadcasted_iota(jnp.int32, (tk, w), 0) + diag * tk
            t_idx = lax.broadcasted_iota(jnp.int32, (tk, w), 1) + st * w
            s = jnp.where(s_idx <= t_idx, s, -jnp.inf)
        return s

    def softmax(m, s):
        m_new = jnp.maximum(m, jnp.max(s, axis=0, keepdims=True))
        return m_new, jnp.exp2(m - m_new), jnp.exp2(s - m_new).astype(BF16)

    def weighted(kj, u, alpha, p, acc):
        hh = units[u][1]
        k0 = pl.multiple_of(kj * tk, tk)
        vt = jnp.concatenate(
            [vt_ref[0, pl.ds(hh * hd, hd), pl.ds(k0, tk)], ones], axis=0)
        return alpha * acc + jnp.dot(vt, p, preferred_element_type=F32)

    def run(work, state):
        la, lb = ATT_SCORE_LEAD, ATT_SCORE_LEAD + ATT_PV_LAG
        state, s_of, sm_of, n = dict(state), {}, {}, len(work)
        assert all(work[i][2] != work[j][2]
                   for i in range(n) for j in range(max(0, i - (lb - la)), i))
        for step in range(n + lb):
            if step < n:
                s_of[step] = scores(*work[step])
            if la <= step < n + la:
                u = work[step - la][2]
                sm_of[step - la] = softmax(state[u][0], s_of.pop(step - la))
            if step >= lb:
                kj, _, u = work[step - lb]
                m_new, alpha, p = sm_of.pop(step - lb)
                state[u] = (m_new, weighted(kj, u, alpha, p, state[u][1]))
        return state

    state = {u: (jnp.full((1, w), -jnp.inf, F32),
                 jnp.zeros((hd + ATT_ONES_ROWS, w), F32)) for u in range(len(units))}

    for qi in range(1, SEQ // tq):
        mine = of_tile(qi)

        def full_tiles(i, carry, mine=mine):
            out = run([(i * ATT_KV_UNROLL + j, None, u)
                       for j in range(ATT_KV_UNROLL) for u in mine], dict(zip(mine, carry)))
            return tuple(out[u] for u in mine)

        carry = lax.fori_loop(0, qi * tiles_per_q // ATT_KV_UNROLL, full_tiles,
                              tuple(state[u] for u in mine))
        state.update(zip(mine, carry))

    per_tile = [[(qi * tiles_per_q + d, d, u) for d in range(tiles_per_q)
                 for u in of_tile(qi) if (units[u][2] + 1) * w > d * tk]
                for qi in range(SEQ // tq)]
    state = run([item for group in zip(*per_tile) for item in group], state)

    for qi in range(SEQ // tq):
        heads = []
        for hh in range(2):
            acc = jnp.concatenate([state[u][1] for u in of_tile(qi)
                                   if units[u][1] == hh], axis=1)
            heads.append(acc[:hd] / acc[hd:hd + 1])
        o = jnp.concatenate(heads, axis=0)
        o_ref[qi * tq:(qi + 1) * tq, :] = o.T.astype(BF16)


def _fox_bias_permutation():
    pairs, hd = FOX_HEADS // 2, FOX_HEAD_DIM
    perm = np.zeros((pairs, LANES, LANES), np.float32)
    for p in range(pairs):
        for hh in range(2):
            for j in range(FOX_BIAS_TERMS):
                perm[p, FOX_GATE_LANES * (2 * p + hh) + j, (1 - hh) * hd + j] = 1.0
    return perm


def _fox_attn(qt, k, ck, vt):
    pairs = FOX_HEADS // 2
    perm = jnp.asarray(_fox_bias_permutation(), BF16)
    return pl.pallas_call(
        _fox_attn_kernel,
        grid=(BATCH, pairs),
        in_specs=[
            pl.BlockSpec((1, LANES, SEQ), lambda b, p: (b, p, 0)),
            pl.BlockSpec((SEQ, LANES), lambda b, p: (b, p)),
            pl.BlockSpec((SEQ, LANES), lambda b, p: (b, 0)),
            pl.BlockSpec((None, LANES, LANES), lambda b, p: (p, 0, 0)),
            pl.BlockSpec((1, LANES, SEQ), lambda b, p: (b, p, 0)),
        ],
        out_specs=pl.BlockSpec((SEQ, LANES), lambda b, p: (b, p)),
        out_shape=jax.ShapeDtypeStruct((TOKENS, D_MODEL), BF16),
        scratch_shapes=[pltpu.VMEM((2, SEQ, LANES), BF16),
                        pltpu.VMEM((2, LANES, SEQ), BF16)],
        compiler_params=_params("parallel", "parallel"),
        name="fox_attn",
    )(qt, k, ck, perm, vt)


def _gla_inproj_kernel(x_ref, g_ref, wq_ref, wk_ref, wv_ref, wr_ref, wzt_ref,
                       wg_ref, bg_ref, q_ref, k_ref, v_ref, r_ref, gate_ref):
    hn = _rms(x_ref[...], g_ref[...]).astype(BF16)
    zt = lax.dot_general(wzt_ref[...], hn, NT_DIMS,
                         preferred_element_type=F32).astype(BF16)
    zg = lax.dot_general(zt, wg_ref[...], TN_DIMS,
                         preferred_element_type=F32) + bg_ref[...]
    gate_ref[...] = _log_sigmoid(zg) * (LOG2E / GLA_GATE_TAU)
    q = jnp.dot(hn, wq_ref[...], preferred_element_type=F32)
    q_ref[...] = q * GLA_SCALE
    k_ref[...] = jnp.dot(hn, wk_ref[...], preferred_element_type=F32)
    v_ref[...] = jnp.dot(hn, wv_ref[...], preferred_element_type=F32).astype(BF16)
    r_ref[...] = jnp.dot(hn, wr_ref[...], preferred_element_type=F32)


def _gla_inproj(h, g, wq, wk, wv, wr, wzt, wg, bg):
    tm = PROJ_TM
    rows = lambda i: (i, 0)
    return pl.pallas_call(
        _gla_inproj_kernel,
        grid=(TOKENS // tm,),
        in_specs=[
            pl.BlockSpec((tm, D_MODEL), rows),
            _resident((1, D_MODEL)),
            _resident((D_MODEL, GLA_KEY_DIM)),
            _resident((D_MODEL, GLA_KEY_DIM)),
            _resident((D_MODEL, D_MODEL)),
            _resident((D_MODEL, D_MODEL)),
            _resident((GLA_GATE_RANK, D_MODEL)),
            _resident((GLA_GATE_RANK, GLA_KEY_DIM)),
            _resident((1, GLA_KEY_DIM)),
        ],
        out_specs=[
            pl.BlockSpec((tm, GLA_KEY_DIM), rows),
            pl.BlockSpec((tm, GLA_KEY_DIM), rows),
            pl.BlockSpec((tm, D_MODEL), rows),
            pl.BlockSpec((tm, D_MODEL), rows),
            pl.BlockSpec((tm, GLA_KEY_DIM), rows),
        ],
        out_shape=[
            jax.ShapeDtypeStruct((TOKENS, GLA_KEY_DIM), F32),
            jax.ShapeDtypeStruct((TOKENS, GLA_KEY_DIM), F32),
            jax.ShapeDtypeStruct((TOKENS, D_MODEL), BF16),
            jax.ShapeDtypeStruct((TOKENS, D_MODEL), F32),
            jax.ShapeDtypeStruct((TOKENS, GLA_KEY_DIM), F32),
        ],
        compiler_params=_params("parallel"),
        name="gla_inproj",
    )(h, g, wq, wk, wv, wr, wzt, wg, bg)


GLA_FINE_LEVELS = (1, 2, 4)
GLA_COARSE_LEVELS = (8, 16, 32)
GLA_LEVELS = GLA_FINE_LEVELS + GLA_COARSE_LEVELS
GLA_STEP_CHUNKS = 8
SUBLANES = 8
assert max(GLA_FINE_LEVELS) < SUBLANES <= min(GLA_COARSE_LEVELS)


def _gla_selection_matrix():
    c = CHUNK
    t = np.arange(c)[:, None]
    r = np.arange(c)[None, :]
    mats = [(r > m * (t // m)) & (r <= t) for m in GLA_FINE_LEVELS[1:]]
    for m in GLA_FINE_LEVELS:
        nxt = np.minimum(m * (t // m + 1), c - 1)
        mats.append((r > t) & (r <= nxt))
    mats.append(r <= t)
    sel = np.concatenate(mats, axis=0).astype(np.float32)
    return np.concatenate([sel, sel], axis=1)


GLA_SEL_ROWS = 2 * len(GLA_FINE_LEVELS) * CHUNK


def _gla_chunk_kernel(q_ref, k_ref, v_ref, r_ref, gate_ref, sel_ref, gn_ref,
                      y_ref, state_ref):
    c, nh, dk, dv = CHUNK, GLA_HEADS, GLA_DK, GLA_DV

    @pl.when(pl.program_id(1) == 0)
    def _():
        state_ref[...] = jnp.zeros_like(state_ref)

    t_idx = lax.broadcasted_iota(jnp.int32, (nh * c, LANES), 0) % c
    l_idx = lax.broadcasted_iota(jnp.int32, (nh * c, LANES), 1)
    h_idx = lax.broadcasted_iota(jnp.int32, (nh * c, LANES), 0) // c
    s_idx = l_idx % c
    x = t_idx ^ s_idx
    level = sum((x >= m).astype(jnp.int32) for m in GLA_LEVELS)
    level = jnp.where((l_idx // c == h_idx % (LANES // c)) & (s_idx <= t_idx), level, -1)

    def stack_heads(a):
        return jnp.concatenate([a[:, h * dk:(h + 1) * dk] for h in range(nh)], axis=0)

    def band(qq, kk):
        full = lax.dot_general(stack_heads(qq.astype(BF16)), stack_heads(kk.astype(BF16)),
                               NT_DIMS, preferred_element_type=F32)
        per_tile = LANES // c
        return jnp.concatenate(
            [full[h * c:(h + 1) * c, (h // per_tile) * LANES:(h // per_tile + 1) * LANES]
             for h in range(nh)], axis=0)

    def blockwise(b, m, side):
        out = []
        for j in range(c // m):
            blk = b[j * m:(j + 1) * m, :]
            if side == "q":
                out.append(blk - b[j * m:j * m + 1, :])
            else:
                nxt = min((j + 1) * m, c - 1)
                out.append(b[nxt:nxt + 1, :] - blk)
        return jnp.concatenate(out, axis=0)

    n_fine = len(GLA_FINE_LEVELS)

    def state_free_part(rows):
        q = q_ref[rows, :]
        k = k_ref[rows, :]
        gate = gate_ref[rows, :]
        g_hi = gate.astype(BF16)
        g_lo = (gate - g_hi.astype(F32)).astype(BF16)
        sums = jnp.dot(sel_ref[...], jnp.concatenate([g_hi, g_lo], axis=0),
                       preferred_element_type=F32)
        fine = jnp.exp2(sums[:(2 * n_fine - 1) * c, :])
        b = sums[(2 * n_fine - 1) * c:, :]
        attn = jnp.where(level == 0, band(q, k), 0.0)
        for li, m in enumerate(GLA_LEVELS):
            if m in GLA_FINE_LEVELS:
                qq = q if li == 0 else q * fine[(li - 1) * c:li * c, :]
                kk = k * fine[(n_fine - 1 + li) * c:(n_fine + li) * c, :]
            else:
                qq = q * jnp.exp2(blockwise(b, m, "q"))
                kk = k * jnp.exp2(blockwise(b, m, "k"))
            attn = jnp.where(level == li + 1, band(qq, kk), attn)
        e_q = jnp.exp2(b)
        qs = (q * e_q).astype(BF16)
        kd = (k * jnp.exp2(b[c - 1:c, :] - b)).astype(BF16)
        return attn.astype(BF16), qs, kd, e_q[c - 1:c, :]

    chunks = [pl.ds(ci * c, c) for ci in range(GLA_STEP_CHUNKS)]
    parts = [state_free_part(rows) for rows in chunks]
    gn = gn_ref[...]
    for rows, (attn, qs, kd, e_last) in zip(chunks, parts):
        for h in range(nh):
            v_h = v_ref[rows, h * dv:(h + 1) * dv]
            state_t = state_ref[h]
            o = jnp.dot(attn[h * c:(h + 1) * c, :],
                        jnp.concatenate([v_h] * (LANES // c), axis=0),
                        preferred_element_type=F32)
            o = o + lax.dot_general(qs[:, h * dk:(h + 1) * dk], state_t.astype(BF16),
                                    NT_DIMS, preferred_element_type=F32)
            state_ref[h] = state_t * e_last[:, h * dk:(h + 1) * dk] + lax.dot_general(
                v_h, kd[:, h * dk:(h + 1) * dk], TN_DIMS, preferred_element_type=F32)
            o = _rms(o, gn)
            r = r_ref[rows, h * dv:(h + 1) * dv]
            y_ref[rows, h * dv:(h + 1) * dv] = (o * (r * jax.nn.sigmoid(r))).astype(BF16)


def _gla_chunks(q, k, v, r, gate, sel, gn):
    tm = GLA_STEP_CHUNKS * CHUNK
    n = SEQ // tm
    rows = lambda b, j: (b * n + j, 0)
    full = lambda b, j: (0, 0)
    return pl.pallas_call(
        _gla_chunk_kernel,
        grid=(BATCH, n),
        in_specs=[
            pl.BlockSpec((tm, GLA_KEY_DIM), rows),
            pl.BlockSpec((tm, GLA_KEY_DIM), rows),
            pl.BlockSpec((tm, D_MODEL), rows),
            pl.BlockSpec((tm, D_MODEL), rows),
            pl.BlockSpec((tm, GLA_KEY_DIM), rows),
            pl.BlockSpec((GLA_SEL_ROWS, 2 * CHUNK), full),
            pl.BlockSpec((1, GLA_DV), full),
        ],
        out_specs=pl.BlockSpec((tm, D_MODEL), rows),
        out_shape=jax.ShapeDtypeStruct((TOKENS, D_MODEL), BF16),
        scratch_shapes=[pltpu.VMEM((GLA_HEADS, GLA_DV, GLA_DK), F32)],
        compiler_params=_params("parallel", "arbitrary"),
        name="gla_chunks",
    )(q, k, v, r, gate, sel, gn)


def _mlp_kernel(a_ref, wo_ref, res_ref, g_ref, wup_ref, wdn_ref, gf_ref, o_ref,
                *, final_norm):
    x = res_ref[...] + jnp.dot(a_ref[...], wo_ref[...], preferred_element_type=F32)
    hn = _rms(x, g_ref[...])
    for f in range(D_FF // MLP_TF):
        cols = slice(f * MLP_TF, (f + 1) * MLP_TF)
        u = jnp.maximum(jnp.dot(hn, wup_ref[:, cols], preferred_element_type=F32), 0.0)
        x = x + jnp.dot(u * u, wdn_ref[cols, :], preferred_element_type=F32)
    o_ref[...] = _rms(x, gf_ref[...]) if final_norm else x


def _mlp(a, w_out, jo, res, g, w_up, w_down, layer, gf, final_norm):
    tm = MLP_TM
    rows = lambda i: (i, 0)

    def layer_weights(shape, idx):
        return pl.BlockSpec((None,) + shape, lambda i: (idx, 0, 0),
                            pipeline_mode=pl.Buffered(1))

    return pl.pallas_call(
        functools.partial(_mlp_kernel, final_norm=final_norm),
        grid=(TOKENS // tm,),
        in_specs=[
            pl.BlockSpec((tm, D_MODEL), rows),
            layer_weights((D_MODEL, D_MODEL), jo),
            pl.BlockSpec((tm, D_MODEL), rows),
            _resident((1, D_MODEL)),
            layer_weights((D_MODEL, D_FF), layer),
            layer_weights((D_FF, D_MODEL), layer),
            _resident((1, D_MODEL)),
        ],
        out_specs=pl.BlockSpec((tm, D_MODEL), rows),
        out_shape=jax.ShapeDtypeStruct((TOKENS, D_MODEL), F32),
        compiler_params=_params("parallel"),
        name="mlp",
    )(a, w_out, res, g, w_up, w_down, gf)


def _row(v):
    return v.reshape(1, -1).astype(F32)


def _fox_mixer(h, g, w_in, b_f):
    d = D_MODEL
    wqt = w_in[:, :d].T.astype(BF16)
    wk = w_in[:, d:2 * d].astype(BF16)
    wvt = w_in[:, 2 * d:3 * d].T.astype(BF16)
    wf = jnp.repeat(w_in[:, 3 * d:], FOX_GATE_LANES, axis=1).astype(BF16)
    bf = _row(jnp.repeat(b_f, FOX_GATE_LANES))
    qt, k, vt, ck = _fox_inproj(h, _row(g), wqt, wk, wvt, wf, bf)
    return _fox_attn(qt, k, ck, vt)


def _gla_mixer(h, g, w_in, w_gate_up, b_gate, g_norm, sel):
    kd, d = GLA_KEY_DIM, D_MODEL
    wq = w_in[:, :kd].astype(BF16)
    wk = w_in[:, kd:2 * kd].astype(BF16)
    wv = w_in[:, 2 * kd:2 * kd + d].astype(BF16)
    z0 = 2 * kd + d
    wzt = w_in[:, z0:z0 + GLA_GATE_RANK].T.astype(BF16)
    wr = w_in[:, z0 + GLA_GATE_RANK:].astype(BF16)
    wg = w_gate_up.astype(BF16)
    q, k, v, r, gate = _gla_inproj(h, _row(g), wq, wk, wv, wr, wzt, wg, _row(b_gate))
    return _gla_chunks(q, k, v, r, gate, sel, _row(g_norm))


def kernel(x, fox_w_in, fox_b_f, fox_w_out, gla_w_in, gla_w_gate_up, gla_b_gate,
           gla_norm_g, gla_w_out, mlp_w_up, mlp_w_down, norm_mix_g, norm_mlp_g,
           norm_final_g):
    assert x.shape == (BATCH, SEQ, D_MODEL) and x.dtype == F32
    sel = jnp.asarray(_gla_selection_matrix(), BF16)
    w_up, w_down = mlp_w_up, mlp_w_down
    w_out = (fox_w_out.astype(BF16), gla_w_out.astype(BF16))
    h = x.reshape(TOKENS, D_MODEL)
    for i in range(DEPTH):
        j = i // 2
        if i % 2 == 0:
            a = _fox_mixer(h, norm_mix_g[i], fox_w_in[j], fox_b_f[j])
        else:
            a = _gla_mixer(h, norm_mix_g[i], gla_w_in[j], gla_w_gate_up[j],
                           gla_b_gate[j], gla_norm_g[j], sel)
        h = _mlp(a, w_out[i % 2], j, h, _row(norm_mlp_g[i]), w_up, w_down, i,
                 _row(norm_final_g), final_norm=(i == DEPTH - 1))
    return h.reshape(BATCH, SEQ, D_MODEL)
```

```python
import functools

import numpy as np
import jax
import jax.numpy as jnp
from jax import lax
from jax.experimental import pallas as pl
from jax.experimental.pallas import tpu as pltpu

F32 = jnp.float32
BF16 = jnp.bfloat16

D_MODEL = 1024
BATCH = 4
SEQ = 4096
TOKENS = BATCH * SEQ
DEPTH = 4
EPS = 1e-6

FOX_HEADS = 16
FOX_HEAD_DIM = D_MODEL // FOX_HEADS
FOX_SCALE = FOX_HEAD_DIM ** -0.5
FOX_GATE_LANES = 8
FOX_BIAS_TERMS = 3
GLA_HEADS = 4
GLA_KEY_DIM = D_MODEL // 2
GLA_DK = GLA_KEY_DIM // GLA_HEADS
GLA_DV = D_MODEL // GLA_HEADS
GLA_GATE_RANK = 16
GLA_GATE_TAU = 16.0
GLA_SCALE = GLA_DK ** -0.5
CHUNK = 64
D_FF = 4 * D_MODEL

LANES = 128
VMEM_LIMIT = 56 * 1024 * 1024

PROJ_TM = 1024
MLP_TM = 512
MLP_TF = 1024
ATT_TQ = 2048
ATT_TK = 512
ATT_KV_UNROLL = 4
ATT_SCORE_LEAD = 2
ATT_PV_LAG = 1
assert ATT_TQ % (ATT_TK * ATT_KV_UNROLL) == 0
ATT_STRIP = 512
ATT_ONES_ROWS = 16
LOG2E = 1.4426950408889634
GATE_BLOCK = 256

NT_DIMS = (((1,), (1,)), ((), ()))
TN_DIMS = (((0,), (0,)), ((), ()))


def _params(*semantics):
    return pltpu.CompilerParams(dimension_semantics=semantics,
                                vmem_limit_bytes=VMEM_LIMIT)


def _resident(shape):
    return pl.BlockSpec(shape, lambda *_: (0,) * len(shape),
                        pipeline_mode=pl.Buffered(1))


def _rms(x, g):
    ms = jnp.mean(x * x, axis=-1, keepdims=True)
    return x * lax.rsqrt(ms + EPS) * g


def _log_sigmoid(x):
    return jnp.minimum(x, 0.0) - jnp.log1p(jnp.exp(-jnp.abs(x)))


def _split3(x):
    hi = x.astype(BF16)
    r1 = x - hi.astype(F32)
    mid = r1.astype(BF16)
    lo = (r1 - mid.astype(F32)).astype(BF16)
    return hi, mid, lo


def _fox_inproj_kernel(x_ref, g_ref, wqt_ref, wk_ref, wvt_ref, wf_ref, bf_ref,
                       qt_ref, k_ref, vt_ref, ck_ref, carry_ref):
    bl = GATE_BLOCK

    @pl.when(pl.program_id(0) % (SEQ // PROJ_TM) == 0)
    def _():
        carry_ref[...] = jnp.zeros_like(carry_ref)

    hn = _rms(x_ref[...], g_ref[...]).astype(BF16)

    qt = lax.dot_general(wqt_ref[...], hn, NT_DIMS, preferred_element_type=F32)
    qt_ref[0] = (qt * (FOX_SCALE * LOG2E)).astype(BF16)

    lf = _log_sigmoid(jnp.dot(hn, wf_ref[...], preferred_element_type=F32) + bf_ref[...])
    row = lax.broadcasted_iota(jnp.int32, (bl, bl), 0)
    col = lax.broadcasted_iota(jnp.int32, (bl, bl), 1)
    tril = (col <= row).astype(BF16)
    j = lax.broadcasted_iota(jnp.int32, (1, LANES), 1) % FOX_GATE_LANES
    carry = carry_ref[...]
    for i in range(PROJ_TM // bl):
        hi, mid, lo = _split3(lf[i * bl:(i + 1) * bl, :])
        c = carry + (jnp.dot(tril, hi, preferred_element_type=F32)
                     + jnp.dot(tril, mid, preferred_element_type=F32)
                     + jnp.dot(tril, lo, preferred_element_type=F32))
        chi, cmid, clo = (t.astype(F32) for t in _split3(c * LOG2E))
        ck = jnp.where(j == 0, -chi, jnp.where(j == 1, -cmid, jnp.where(
            j == 2, -clo, 0.0)))
        ck_ref[i * bl:(i + 1) * bl, :] = ck.astype(BF16)
        carry = c[bl - 1:bl, :]
    carry_ref[...] = carry

    k_ref[...] = jnp.dot(hn, wk_ref[...], preferred_element_type=F32).astype(BF16)
    vt = lax.dot_general(wvt_ref[...], hn, NT_DIMS, preferred_element_type=F32)
    vt_ref[0] = vt.astype(BF16)


def _fox_inproj(h, g, wqt, wk, wvt, wf, bf):
    tm = PROJ_TM
    per_b = SEQ // tm
    return pl.pallas_call(
        _fox_inproj_kernel,
        grid=(TOKENS // tm,),
        in_specs=[
            pl.BlockSpec((tm, D_MODEL), lambda i: (i, 0)),
            _resident((1, D_MODEL)),
            _resident((D_MODEL, D_MODEL)),
            _resident((D_MODEL, D_MODEL)),
            _resident((D_MODEL, D_MODEL)),
            _resident((D_MODEL, LANES)),
            _resident((1, LANES)),
        ],
        out_specs=[
            pl.BlockSpec((1, D_MODEL, tm), lambda i: (i // per_b, 0, i % per_b)),
            pl.BlockSpec((tm, D_MODEL), lambda i: (i, 0)),
            pl.BlockSpec((1, D_MODEL, tm), lambda i: (i // per_b, 0, i % per_b)),
            pl.BlockSpec((tm, LANES), lambda i: (i, 0)),
        ],
        out_shape=[
            jax.ShapeDtypeStruct((BATCH, D_MODEL, SEQ), BF16),
            jax.ShapeDtypeStruct((TOKENS, D_MODEL), BF16),
            jax.ShapeDtypeStruct((BATCH, D_MODEL, SEQ), BF16),
            jax.ShapeDtypeStruct((TOKENS, LANES), BF16),
        ],
        scratch_shapes=[pltpu.VMEM((1, LANES), F32)],
        compiler_params=_params("arbitrary"),
        name="fox_inproj",
    )(h, g, wqt, wk, wvt, wf, bf)


def _fox_attn_kernel(qt_ref, k_ref, ck_ref, perm_ref, vt_ref, o_ref, kcat_ref, qcat_ref):
    tq, tk, hd, w = ATT_TQ, ATT_TK, FOX_HEAD_DIM, ATT_STRIP

    bias = jnp.dot(ck_ref[...], perm_ref[...], preferred_element_type=F32)
    keys = k_ref[...].astype(F32)
    lane_half = lax.broadcasted_iota(jnp.int32, (SEQ, LANES), 1) // hd
    queries = qt_ref[0].astype(F32)
    row = lax.broadcasted_iota(jnp.int32, (LANES, SEQ), 0)
    for hh in range(2):
        kcat_ref[hh] = jnp.where(lane_half == hh, keys, bias).astype(BF16)
        ones_at = (1 - hh) * hd
        ones_rows = ((row >= ones_at) & (row < ones_at + FOX_BIAS_TERMS)).astype(F32)
        qcat_ref[hh] = jnp.where(row // hd == hh, queries, ones_rows).astype(BF16)

    strips, tiles_per_q = tq // w, tq // tk
    units = [(qi, hh, st) for qi in range(SEQ // tq) for st in range(strips)
             for hh in range(2)]
    of_tile = lambda qi: [u for u in range(len(units)) if units[u][0] == qi]
    ones = jnp.ones((ATT_ONES_ROWS, tk), BF16)

    def scores(kj, diag, u):
        qi, hh, st = units[u]
        k0 = pl.multiple_of(kj * tk, tk)
        s = jnp.dot(kcat_ref[hh, pl.ds(k0, tk), :],
                    qcat_ref[hh, :, pl.ds(qi * tq + st * w, w)],
                    preferred_element_type=F32)
        if diag is not None and (diag + 1) * tk > st * w + 1:
            s_idx = lax.broadcasted_iota(jnp.int32, (tk, w), 0) + diag * tk
            t_idx = lax.broadcasted_iota(jnp.int32, (tk, w), 1) + st * w
            s = jnp.where(s_idx <= t_idx, s, -jnp.inf)
        return s

    def softmax(m, s):
        m_new = jnp.maximum(m, jnp.max(s, axis=0, keepdims=True))
        return m_new, jnp.exp2(m - m_new), jnp.exp2(s - m_new).astype(BF16)

    def weighted(kj, u, alpha, p, acc):
        hh = units[u][1]
        k0 = pl.multiple_of(kj * tk, tk)
        vt = jnp.concatenate(
            [vt_ref[0, pl.ds(hh * hd, hd), pl.ds(k0, tk)], ones], axis=0)
        return alpha * acc + jnp.dot(vt, p, preferred_element_type=F32)

    def run(work, state):
        la, lb = ATT_SCORE_LEAD, ATT_SCORE_LEAD + ATT_PV_LAG
        state, s_of, sm_of, n = dict(state), {}, {}, len(work)
        assert all(work[i][2] != work[j][2]
                   for i in range(n) for j in range(max(0, i - (lb - la)), i))
        for step in range(n + lb):
            if step < n:
                s_of[step] = scores(*work[step])
            if la <= step < n + la:
                u = work[step - la][2]
                sm_of[step - la] = softmax(state[u][0], s_of.pop(step - la))
            if step >= lb:
                kj, _, u = work[step - lb]
                m_new, alpha, p = sm_of.pop(step - lb)
                state[u] = (m_new, weighted(kj, u, alpha, p, state[u][1]))
        return state

    state = {u: (jnp.full((1, w), -jnp.inf, F32),
                 jnp.zeros((hd + ATT_ONES_ROWS, w), F32)) for u in range(len(units))}

    for qi in range(1, SEQ // tq):
        mine = of_tile(qi)

        def full_tiles(i, carry, mine=mine):
            out = run([(i * ATT_KV_UNROLL + j, None, u)
                       for j in range(ATT_KV_UNROLL) for u in mine], dict(zip(mine, carry)))
            return tuple(out[u] for u in mine)

        trips = qi * tiles_per_q // ATT_KV_UNROLL
        carry = tuple(state[u] for u in mine)
        carry = (full_tiles(0, carry) if trips == 1
                 else lax.fori_loop(0, trips, full_tiles, carry))
        state.update(zip(mine, carry))

    per_tile = [[(qi * tiles_per_q + d, d, u) for d in range(tiles_per_q)
                 for u in of_tile(qi) if (units[u][2] + 1) * w > d * tk]
                for qi in range(SEQ // tq)]
    state = run([item for group in zip(*per_tile) for item in group], state)

    for qi in range(SEQ // tq):
        heads = []
        for hh in range(2):
            acc = jnp.concatenate([state[u][1] for u in of_tile(qi)
                                   if units[u][1] == hh], axis=1)
            heads.append(acc[:hd] / acc[hd:hd + 1])
        o = jnp.concatenate(heads, axis=0)
        o_ref[qi * tq:(qi + 1) * tq, :] = o.T.astype(BF16)


def _fox_bias_permutation():
    pairs, hd = FOX_HEADS // 2, FOX_HEAD_DIM
    perm = np.zeros((pairs, LANES, LANES), np.float32)
    for p in range(pairs):
        for hh in range(2):
            for j in range(FOX_BIAS_TERMS):
                perm[p, FOX_GATE_LANES * (2 * p + hh) + j, (1 - hh) * hd + j] = 1.0
    return perm


def _fox_attn(qt, k, ck, vt):
    pairs = FOX_HEADS // 2
    perm = jnp.asarray(_fox_bias_permutation(), BF16)
    return pl.pallas_call(
        _fox_attn_kernel,
        grid=(BATCH, pairs),
        in_specs=[
            pl.BlockSpec((1, LANES, SEQ), lambda b, p: (b, p, 0)),
            pl.BlockSpec((SEQ, LANES), lambda b, p: (b, p)),
            pl.BlockSpec((SEQ, LANES), lambda b, p: (b, 0)),
            pl.BlockSpec((None, LANES, LANES), lambda b, p: (p, 0, 0)),
            pl.BlockSpec((1, LANES, SEQ), lambda b, p: (b, p, 0)),
        ],
        out_specs=pl.BlockSpec((SEQ, LANES), lambda b, p: (b, p)),
        out_shape=jax.ShapeDtypeStruct((TOKENS, D_MODEL), BF16),
        scratch_shapes=[pltpu.VMEM((2, SEQ, LANES), BF16),
                        pltpu.VMEM((2, LANES, SEQ), BF16)],
        compiler_params=_params("parallel", "parallel"),
        name="fox_attn",
    )(qt, k, ck, perm, vt)


def _gla_inproj_kernel(x_ref, g_ref, wq_ref, wk_ref, wv_ref, wr_ref, wzt_ref,
                       wg_ref, bg_ref, q_ref, k_ref, v_ref, r_ref, gate_ref):
    hn = _rms(x_ref[...], g_ref[...]).astype(BF16)
    zt = lax.dot_general(wzt_ref[...], hn, NT_DIMS,
                         preferred_element_type=F32).astype(BF16)
    zg = lax.dot_general(zt, wg_ref[...], TN_DIMS,
                         preferred_element_type=F32) + bg_ref[...]
    gate_ref[...] = _log_sigmoid(zg) * (LOG2E / GLA_GATE_TAU)
    q = jnp.dot(hn, wq_ref[...], preferred_element_type=F32)
    q_ref[...] = q * GLA_SCALE
    k_ref[...] = jnp.dot(hn, wk_ref[...], preferred_element_type=F32)
    v_ref[...] = jnp.dot(hn, wv_ref[...], preferred_element_type=F32).astype(BF16)
    r_ref[...] = jnp.dot(hn, wr_ref[...], preferred_element_type=F32)


def _gla_inproj(h, g, wq, wk, wv, wr, wzt, wg, bg):
    tm = PROJ_TM
    rows = lambda i: (i, 0)
    return pl.pallas_call(
        _gla_inproj_kernel,
        grid=(TOKENS // tm,),
        in_specs=[
            pl.BlockSpec((tm, D_MODEL), rows),
            _resident((1, D_MODEL)),
            _resident((D_MODEL, GLA_KEY_DIM)),
            _resident((D_MODEL, GLA_KEY_DIM)),
            _resident((D_MODEL, D_MODEL)),
            _resident((D_MODEL, D_MODEL)),
            _resident((GLA_GATE_RANK, D_MODEL)),
            _resident((GLA_GATE_RANK, GLA_KEY_DIM)),
            _resident((1, GLA_KEY_DIM)),
        ],
        out_specs=[
            pl.BlockSpec((tm, GLA_KEY_DIM), rows),
            pl.BlockSpec((tm, GLA_KEY_DIM), rows),
            pl.BlockSpec((tm, D_MODEL), rows),
            pl.BlockSpec((tm, D_MODEL), rows),
            pl.BlockSpec((tm, GLA_KEY_DIM), rows),
        ],
        out_shape=[
            jax.ShapeDtypeStruct((TOKENS, GLA_KEY_DIM), F32),
            jax.ShapeDtypeStruct((TOKENS, GLA_KEY_DIM), F32),
            jax.ShapeDtypeStruct((TOKENS, D_MODEL), BF16),
            jax.ShapeDtypeStruct((TOKENS, D_MODEL), F32),
            jax.ShapeDtypeStruct((TOKENS, GLA_KEY_DIM), F32),
        ],
        compiler_params=_params("parallel"),
        name="gla_inproj",
    )(h, g, wq, wk, wv, wr, wzt, wg, bg)


GLA_FINE_LEVELS = (1, 2, 4)
GLA_COARSE_LEVELS = (8, 16, 32)
GLA_LEVELS = GLA_FINE_LEVELS + GLA_COARSE_LEVELS
GLA_STEP_CHUNKS = 8
SUBLANES = 8
assert max(GLA_FINE_LEVELS) < SUBLANES <= min(GLA_COARSE_LEVELS)


def _gla_selection_matrix():
    c = CHUNK
    t = np.arange(c)[:, None]
    r = np.arange(c)[None, :]
    mats = [(r > m * (t // m)) & (r <= t) for m in GLA_FINE_LEVELS[1:]]
    for m in GLA_FINE_LEVELS:
        nxt = np.minimum(m * (t // m + 1), c - 1)
        mats.append((r > t) & (r <= nxt))
    mats.append(r <= t)
    sel = np.concatenate(mats, axis=0).astype(np.float32)
    return np.concatenate([sel, sel], axis=1)


GLA_SEL_ROWS = 2 * len(GLA_FINE_LEVELS) * CHUNK


def _gla_chunk_kernel(q_ref, k_ref, v_ref, r_ref, gate_ref, sel_ref, gn_ref,
                      y_ref, state_ref):
    c, nh, dk, dv = CHUNK, GLA_HEADS, GLA_DK, GLA_DV

    @pl.when(pl.program_id(1) == 0)
    def _():
        state_ref[...] = jnp.zeros_like(state_ref)

    t_idx = lax.broadcasted_iota(jnp.int32, (nh * c, LANES), 0) % c
    l_idx = lax.broadcasted_iota(jnp.int32, (nh * c, LANES), 1)
    h_idx = lax.broadcasted_iota(jnp.int32, (nh * c, LANES), 0) // c
    s_idx = l_idx % c
    x = t_idx ^ s_idx
    level = sum((x >= m).astype(jnp.int32) for m in GLA_LEVELS)
    level = jnp.where((l_idx // c == h_idx % (LANES // c)) & (s_idx <= t_idx), level, -1)

    def stack_heads(a):
        return jnp.concatenate([a[:, h * dk:(h + 1) * dk] for h in range(nh)], axis=0)

    def band(qq, kk):
        full = lax.dot_general(stack_heads(qq.astype(BF16)), stack_heads(kk.astype(BF16)),
                               NT_DIMS, preferred_element_type=F32)
        per_tile = LANES // c
        return jnp.concatenate(
            [full[h * c:(h + 1) * c, (h // per_tile) * LANES:(h // per_tile + 1) * LANES]
             for h in range(nh)], axis=0)

    def blockwise(b, m, side):
        out = []
        for j in range(c // m):
            blk = b[j * m:(j + 1) * m, :]
            if side == "q":
                out.append(blk - b[j * m:j * m + 1, :])
            else:
                nxt = min((j + 1) * m, c - 1)
                out.append(b[nxt:nxt + 1, :] - blk)
        return jnp.concatenate(out, axis=0)

    n_fine = len(GLA_FINE_LEVELS)

    def state_free_part(rows):
        q = q_ref[rows, :]
        k = k_ref[rows, :]
        gate = gate_ref[rows, :]
        g_hi = gate.astype(BF16)
        g_lo = (gate - g_hi.astype(F32)).astype(BF16)
        sums = jnp.dot(sel_ref[...], jnp.concatenate([g_hi, g_lo], axis=0),
                       preferred_element_type=F32)
        fine = jnp.exp2(sums[:(2 * n_fine - 1) * c, :])
        b = sums[(2 * n_fine - 1) * c:, :]
        attn = jnp.where(level == 0, band(q, k), 0.0)
        for li, m in enumerate(GLA_LEVELS):
            if m in GLA_FINE_LEVELS:
                qq = q if li == 0 else q * fine[(li - 1) * c:li * c, :]
                kk = k * fine[(n_fine - 1 + li) * c:(n_fine + li) * c, :]
            else:
                qq = q * jnp.exp2(blockwise(b, m, "q"))
                kk = k * jnp.exp2(blockwise(b, m, "k"))
            attn = jnp.where(level == li + 1, band(qq, kk), attn)
        e_q = jnp.exp2(b)
        qs = (q * e_q).astype(BF16)
        kd = (k * jnp.exp2(b[c - 1:c, :] - b)).astype(BF16)
        return attn.astype(BF16), qs, kd, e_q[c - 1:c, :]

    chunks = [pl.ds(ci * c, c) for ci in range(GLA_STEP_CHUNKS)]
    parts = [state_free_part(rows) for rows in chunks]
    gn = gn_ref[...]
    for rows, (attn, qs, kd, e_last) in zip(chunks, parts):
        for h in range(nh):
            v_h = v_ref[rows, h * dv:(h + 1) * dv]
            state_t = state_ref[h]
            o = jnp.dot(attn[h * c:(h + 1) * c, :],
                        jnp.concatenate([v_h] * (LANES // c), axis=0),
                        preferred_element_type=F32)
            o = o + lax.dot_general(qs[:, h * dk:(h + 1) * dk], state_t.astype(BF16),
                                    NT_DIMS, preferred_element_type=F32)
            state_ref[h] = state_t * e_last[:, h * dk:(h + 1) * dk] + lax.dot_general(
                v_h, kd[:, h * dk:(h + 1) * dk], TN_DIMS, preferred_element_type=F32)
            o = _rms(o, gn)
            r = r_ref[rows, h * dv:(h + 1) * dv]
            y_ref[rows, h * dv:(h + 1) * dv] = (o * (r * jax.nn.sigmoid(r))).astype(BF16)


def _gla_chunks(q, k, v, r, gate, sel, gn):
    tm = GLA_STEP_CHUNKS * CHUNK
    n = SEQ // tm
    rows = lambda b, j: (b * n + j, 0)
    full = lambda b, j: (0, 0)
    return pl.pallas_call(
        _gla_chunk_kernel,
        grid=(BATCH, n),
        in_specs=[
            pl.BlockSpec((tm, GLA_KEY_DIM), rows),
            pl.BlockSpec((tm, GLA_KEY_DIM), rows),
            pl.BlockSpec((tm, D_MODEL), rows),
            pl.BlockSpec((tm, D_MODEL), rows),
            pl.BlockSpec((tm, GLA_KEY_DIM), rows),
            pl.BlockSpec((GLA_SEL_ROWS, 2 * CHUNK), full),
            pl.BlockSpec((1, GLA_DV), full),
        ],
        out_specs=pl.BlockSpec((tm, D_MODEL), rows),
        out_shape=jax.ShapeDtypeStruct((TOKENS, D_MODEL), BF16),
        scratch_shapes=[pltpu.VMEM((GLA_HEADS, GLA_DV, GLA_DK), F32)],
        compiler_params=_params("parallel", "arbitrary"),
        name="gla_chunks",
    )(q, k, v, r, gate, sel, gn)


def _mlp_kernel(a_ref, wo_ref, res_ref, g_ref, wup_ref, wdn_ref, gf_ref, o_ref,
                *, final_norm):
    x = res_ref[...] + jnp.dot(a_ref[...], wo_ref[...], preferred_element_type=F32)
    hn = _rms(x, g_ref[...])
    for f in range(D_FF // MLP_TF):
        cols = slice(f * MLP_TF, (f + 1) * MLP_TF)
        u = jnp.maximum(jnp.dot(hn, wup_ref[:, cols], preferred_element_type=F32), 0.0)
        x = x + jnp.dot(u * u, wdn_ref[cols, :], preferred_element_type=F32)
    o_ref[...] = _rms(x, gf_ref[...]) if final_norm else x


def _mlp(a, w_out, jo, res, g, w_up, w_down, layer, gf, final_norm):
    tm = MLP_TM
    rows = lambda i: (i, 0)

    def layer_weights(shape, idx):
        return pl.BlockSpec((None,) + shape, lambda i: (idx, 0, 0),
                            pipeline_mode=pl.Buffered(1))

    return pl.pallas_call(
        functools.partial(_mlp_kernel, final_norm=final_norm),
        grid=(TOKENS // tm,),
        in_specs=[
            pl.BlockSpec((tm, D_MODEL), rows),
            layer_weights((D_MODEL, D_MODEL), jo),
            pl.BlockSpec((tm, D_MODEL), rows),
            _resident((1, D_MODEL)),
            layer_weights((D_MODEL, D_FF), layer),
            layer_weights((D_FF, D_MODEL), layer),
            _resident((1, D_MODEL)),
        ],
        out_specs=pl.BlockSpec((tm, D_MODEL), rows),
        out_shape=jax.ShapeDtypeStruct((TOKENS, D_MODEL), F32),
        compiler_params=_params("parallel"),
        name="mlp",
    )(a, w_out, res, g, w_up, w_down, gf)


def _row(v):
    return v.reshape(1, -1).astype(F32)


def _fox_mixer(h, g, w_in, b_f):
    d = D_MODEL
    wqt = w_in[:, :d].T.astype(BF16)
    wk = w_in[:, d:2 * d].astype(BF16)
    wvt = w_in[:, 2 * d:3 * d].T.astype(BF16)
    wf = jnp.repeat(w_in[:, 3 * d:], FOX_GATE_LANES, axis=1).astype(BF16)
    bf = _row(jnp.repeat(b_f, FOX_GATE_LANES))
    qt, k, vt, ck = _fox_inproj(h, _row(g), wqt, wk, wvt, wf, bf)
    return _fox_attn(qt, k, ck, vt)


def _gla_mixer(h, g, w_in, w_gate_up, b_gate, g_norm, sel):
    kd, d = GLA_KEY_DIM, D_MODEL
    wq = w_in[:, :kd].astype(BF16)
    wk = w_in[:, kd:2 * kd].astype(BF16)
    wv = w_in[:, 2 * kd:2 * kd + d].astype(BF16)
    z0 = 2 * kd + d
    wzt = w_in[:, z0:z0 + GLA_GATE_RANK].T.astype(BF16)
    wr = w_in[:, z0 + GLA_GATE_RANK:].astype(BF16)
    wg = w_gate_up.astype(BF16)
    q, k, v, r, gate = _gla_inproj(h, _row(g), wq, wk, wv, wr, wzt, wg, _row(b_gate))
    return _gla_chunks(q, k, v, r, gate, sel, _row(g_norm))


def kernel(x, fox_w_in, fox_b_f, fox_w_out, gla_w_in, gla_w_gate_up, gla_b_gate,
           gla_norm_g, gla_w_out, mlp_w_up, mlp_w_down, norm_mix_g, norm_mlp_g,
           norm_final_g):
    assert x.shape == (BATCH, SEQ, D_MODEL) and x.dtype == F32
    sel = jnp.asarray(_gla_selection_matrix(), BF16)
    w_up, w_down = mlp_w_up, mlp_w_down
    w_out = (fox_w_out.astype(BF16), gla_w_out.astype(BF16))
    h = x.reshape(TOKENS, D_MODEL)
    for i in range(DEPTH):
        j = i // 2
        if i % 2 == 0:
            a = _fox_mixer(h, norm_mix_g[i], fox_w_in[j], fox_b_f[j])
        else:
            a = _gla_mixer(h, norm_mix_g[i], gla_w_in[j], gla_w_gate_up[j],
                           gla_b_gate[j], gla_norm_g[j], sel)
        h = _mlp(a, w_out[i % 2], j, h, _row(norm_mlp_g[i]), w_up, w_down, i,
                 _row(norm_final_g), final_norm=(i == DEPTH - 1))
    return h.reshape(BATCH, SEQ, D_MODEL)
```

```python
import functools

import numpy as np
import jax
import jax.numpy as jnp
from jax import lax
from jax.experimental import pallas as pl
from jax.experimental.pallas import tpu as pltpu

F32 = jnp.float32
BF16 = jnp.bfloat16

D_MODEL = 1024
BATCH = 4
SEQ = 4096
TOKENS = BATCH * SEQ
DEPTH = 4
EPS = 1e-6

FOX_HEADS = 16
FOX_HEAD_DIM = D_MODEL // FOX_HEADS
FOX_SCALE = FOX_HEAD_DIM ** -0.5
FOX_GATE_LANES = 8
FOX_BIAS_TERMS = 3
GLA_HEADS = 4
GLA_KEY_DIM = D_MODEL // 2
GLA_DK = GLA_KEY_DIM // GLA_HEADS
GLA_DV = D_MODEL // GLA_HEADS
GLA_GATE_RANK = 16
GLA_GATE_TAU = 16.0
GLA_SCALE = GLA_DK ** -0.5
CHUNK = 64
D_FF = 4 * D_MODEL

LANES = 128
VMEM_LIMIT = 56 * 1024 * 1024

PROJ_TM = 1024
MLP_TM = 512
MLP_TF = 1024
ATT_TQ = 2048
ATT_TK = 512
ATT_KV_UNROLL = 4
ATT_SCORE_LEAD = 2
ATT_PV_LAG = 1
assert ATT_TQ % (ATT_TK * ATT_KV_UNROLL) == 0
ATT_STRIP = 512
ATT_ONES_ROWS = 16
LOG2E = 1.4426950408889634
GATE_BLOCK = 256

NT_DIMS = (((1,), (1,)), ((), ()))
TN_DIMS = (((0,), (0,)), ((), ()))


def _params(*semantics):
    return pltpu.CompilerParams(dimension_semantics=semantics,
                                vmem_limit_bytes=VMEM_LIMIT)


def _resident(shape):
    return pl.BlockSpec(shape, lambda *_: (0,) * len(shape),
                        pipeline_mode=pl.Buffered(1))


def _rms(x, g):
    ms = jnp.mean(x * x, axis=-1, keepdims=True)
    return x * lax.rsqrt(ms + EPS) * g


def _log_sigmoid(x):
    return jnp.minimum(x, 0.0) - jnp.log1p(jnp.exp(-jnp.abs(x)))


def _split3(x):
    hi = x.astype(BF16)
    r1 = x - hi.astype(F32)
    mid = r1.astype(BF16)
    lo = (r1 - mid.astype(F32)).astype(BF16)
    return hi, mid, lo


def _fox_inproj_kernel(x_ref, g_ref, wqt_ref, wk_ref, wvt_ref, wf_ref, bf_ref,
                       qt_ref, k_ref, vt_ref, ck_ref, carry_ref):
    bl = GATE_BLOCK

    @pl.when(pl.program_id(0) % (SEQ // PROJ_TM) == 0)
    def _():
        carry_ref[...] = jnp.zeros_like(carry_ref)

    hn = _rms(x_ref[...], g_ref[...]).astype(BF16)

    qt = lax.dot_general(wqt_ref[...], hn, NT_DIMS, preferred_element_type=F32)
    qt_ref[0] = (qt * (FOX_SCALE * LOG2E)).astype(BF16)

    lf = _log_sigmoid(jnp.dot(hn, wf_ref[...], preferred_element_type=F32) + bf_ref[...])
    row = lax.broadcasted_iota(jnp.int32, (bl, bl), 0)
    col = lax.broadcasted_iota(jnp.int32, (bl, bl), 1)
    tril = (col <= row).astype(BF16)
    j = lax.broadcasted_iota(jnp.int32, (1, LANES), 1) % FOX_GATE_LANES
    carry = carry_ref[...]
    for i in range(PROJ_TM // bl):
        hi, mid, lo = _split3(lf[i * bl:(i + 1) * bl, :])
        c = carry + (jnp.dot(tril, hi, preferred_element_type=F32)
                     + jnp.dot(tril, mid, preferred_element_type=F32)
                     + jnp.dot(tril, lo, preferred_element_type=F32))
        chi, cmid, clo = (t.astype(F32) for t in _split3(c * LOG2E))
        ck = jnp.where(j == 0, -chi, jnp.where(j == 1, -cmid, jnp.where(
            j == 2, -clo, 0.0)))
        ck_ref[i * bl:(i + 1) * bl, :] = ck.astype(BF16)
        carry = c[bl - 1:bl, :]
    carry_ref[...] = carry

    k_ref[...] = jnp.dot(hn, wk_ref[...], preferred_element_type=F32).astype(BF16)
    vt = lax.dot_general(wvt_ref[...], hn, NT_DIMS, preferred_element_type=F32)
    vt_ref[0] = vt.astype(BF16)


def _fox_inproj(h, g, wqt, wk, wvt, wf, bf):
    tm = PROJ_TM
    per_b = SEQ // tm
    return pl.pallas_call(
        _fox_inproj_kernel,
        grid=(TOKENS // tm,),
        in_specs=[
            pl.BlockSpec((tm, D_MODEL), lambda i: (i, 0)),
            _resident((1, D_MODEL)),
            _resident((D_MODEL, D_MODEL)),
            _resident((D_MODEL, D_MODEL)),
            _resident((D_MODEL, D_MODEL)),
            _resident((D_MODEL, LANES)),
            _resident((1, LANES)),
        ],
        out_specs=[
            pl.BlockSpec((1, D_MODEL, tm), lambda i: (i // per_b, 0, i % per_b)),
            pl.BlockSpec((tm, D_MODEL), lambda i: (i, 0)),
            pl.BlockSpec((1, D_MODEL, tm), lambda i: (i // per_b, 0, i % per_b)),
            pl.BlockSpec((tm, LANES), lambda i: (i, 0)),
        ],
        out_shape=[
            jax.ShapeDtypeStruct((BATCH, D_MODEL, SEQ), BF16),
            jax.ShapeDtypeStruct((TOKENS, D_MODEL), BF16),
            jax.ShapeDtypeStruct((BATCH, D_MODEL, SEQ), BF16),
            jax.ShapeDtypeStruct((TOKENS, LANES), BF16),
        ],
        scratch_shapes=[pltpu.VMEM((1, LANES), F32)],
        compiler_params=_params("arbitrary"),
        name="fox_inproj",
    )(h, g, wqt, wk, wvt, wf, bf)


def _fox_attn_kernel(qt_ref, k_ref, ck_ref, perm_ref, vt_ref, o_ref, kcat_ref, qcat_ref):
    tq, tk, hd, w = ATT_TQ, ATT_TK, FOX_HEAD_DIM, ATT_STRIP

    nb = FOX_BIAS_TERMS
    bias = jnp.dot(ck_ref[...], perm_ref[...], preferred_element_type=F32)
    keys = k_ref[...].astype(F32)
    lane = lax.broadcasted_iota(jnp.int32, (SEQ, LANES), 1)
    key_side = jnp.where(lane % hd < nb, bias, jnp.where(lane % hd < 2 * nb, 1.0, 0.0))
    queries = qt_ref[0].astype(F32)
    row = lax.broadcasted_iota(jnp.int32, (LANES, SEQ), 0)
    query_side = jnp.where(row % hd < nb, 1.0,
                           jnp.where(row % hd < 2 * nb, -bias.T, 0.0))
    for hh in range(2):
        kcat_ref[hh] = jnp.where(lane // hd == hh, keys, key_side).astype(BF16)
        qcat_ref[hh] = jnp.where(row // hd == hh, queries, query_side).astype(BF16)

    strips, tiles_per_q = tq // w, tq // tk
    units = [(qi, hh, st) for qi in range(SEQ // tq) for st in range(strips)
             for hh in range(2)]
    of_tile = lambda qi: [u for u in range(len(units)) if units[u][0] == qi]
    ones = jnp.ones((ATT_ONES_ROWS, tk), BF16)

    def scores(kj, diag, u):
        qi, hh, st = units[u]
        k0 = pl.multiple_of(kj * tk, tk)
        s = jnp.dot(kcat_ref[hh, pl.ds(k0, tk), :],
                    qcat_ref[hh, :, pl.ds(qi * tq + st * w, w)],
                    preferred_element_type=F32)
        if diag is not None and (diag + 1) * tk > st * w + 1:
            s_idx = lax.broadcasted_iota(jnp.int32, (tk, w), 0) + diag * tk
            t_idx = lax.broadcasted_iota(jnp.int32, (tk, w), 1) + st * w
            s = jnp.where(s_idx <= t_idx, s, -jnp.inf)
        return s

    def softmax(m, s):
        m_new = jnp.maximum(m, jnp.max(s, axis=0, keepdims=True))
        return m_new, jnp.exp2(m - m_new), jnp.exp2(s - m_new).astype(BF16)

    def weighted(kj, u, alpha, p, acc):
        hh = units[u][1]
        k0 = pl.multiple_of(kj * tk, tk)
        vt = jnp.concatenate(
            [vt_ref[0, pl.ds(hh * hd, hd), pl.ds(k0, tk)], ones], axis=0)
        return alpha * acc + jnp.dot(vt, p, preferred_element_type=F32)

    def run(work, state):
        la, lb = ATT_SCORE_LEAD, ATT_SCORE_LEAD + ATT_PV_LAG
        state, s_of, sm_of, n = dict(state), {}, {}, len(work)
        assert all(work[i][2] != work[j][2]
                   for i in range(n) for j in range(max(0, i - (lb - la)), i))
        for step in range(n + lb):
            if step < n:
                s_of[step] = scores(*work[step])
            if la <= step < n + la:
                u = work[step - la][2]
                sm_of[step - la] = softmax(state[u][0], s_of.pop(step - la))
            if step >= lb:
                kj, _, u = work[step - lb]
                m_new, alpha, p = sm_of.pop(step - lb)
                state[u] = (m_new, weighted(kj, u, alpha, p, state[u][1]))
        return state

    state = {u: (jnp.full((1, w), -jnp.inf, F32),
                 jnp.zeros((hd + ATT_ONES_ROWS, w), F32)) for u in range(len(units))}

    for qi in range(1, SEQ // tq):
        mine = of_tile(qi)

        def full_tiles(i, carry, mine=mine):
            out = run([(i * ATT_KV_UNROLL + j, None, u)
                       for j in range(ATT_KV_UNROLL) for u in mine], dict(zip(mine, carry)))
            return tuple(out[u] for u in mine)

        trips = qi * tiles_per_q // ATT_KV_UNROLL
        carry = tuple(state[u] for u in mine)
        carry = (full_tiles(0, carry) if trips == 1
                 else lax.fori_loop(0, trips, full_tiles, carry))
        state.update(zip(mine, carry))

    per_tile = [[(qi * tiles_per_q + d, d, u) for d in range(tiles_per_q)
                 for u in of_tile(qi) if (units[u][2] + 1) * w > d * tk]
                for qi in range(SEQ // tq)]
    state = run([item for group in zip(*per_tile) for item in group], state)

    for qi in range(SEQ // tq):
        heads = []
        for hh in range(2):
            acc = jnp.concatenate([state[u][1] for u in of_tile(qi)
                                   if units[u][1] == hh], axis=1)
            heads.append(acc[:hd] / acc[hd:hd + 1])
        o = jnp.concatenate(heads, axis=0)
        o_ref[qi * tq:(qi + 1) * tq, :] = o.T.astype(BF16)


def _fox_bias_permutation():
    pairs, hd = FOX_HEADS // 2, FOX_HEAD_DIM
    perm = np.zeros((pairs, LANES, LANES), np.float32)
    for p in range(pairs):
        for hh in range(2):
            for j in range(FOX_BIAS_TERMS):
                src = FOX_GATE_LANES * (2 * p + hh) + j
                perm[p, src, (1 - hh) * hd + j] = 1.0
                perm[p, src, (1 - hh) * hd + FOX_BIAS_TERMS + j] = 1.0
    return perm


def _fox_attn(qt, k, ck, vt):
    pairs = FOX_HEADS // 2
    perm = jnp.asarray(_fox_bias_permutation(), BF16)
    return pl.pallas_call(
        _fox_attn_kernel,
        grid=(BATCH, pairs),
        in_specs=[
            pl.BlockSpec((1, LANES, SEQ), lambda b, p: (b, p, 0)),
            pl.BlockSpec((SEQ, LANES), lambda b, p: (b, p)),
            pl.BlockSpec((SEQ, LANES), lambda b, p: (b, 0)),
            pl.BlockSpec((None, LANES, LANES), lambda b, p: (p, 0, 0)),
            pl.BlockSpec((1, LANES, SEQ), lambda b, p: (b, p, 0)),
        ],
        out_specs=pl.BlockSpec((SEQ, LANES), lambda b, p: (b, p)),
        out_shape=jax.ShapeDtypeStruct((TOKENS, D_MODEL), BF16),
        scratch_shapes=[pltpu.VMEM((2, SEQ, LANES), BF16),
                        pltpu.VMEM((2, LANES, SEQ), BF16)],
        compiler_params=_params("parallel", "parallel"),
        name="fox_attn",
    )(qt, k, ck, perm, vt)


def _gla_inproj_kernel(x_ref, g_ref, wq_ref, wk_ref, wv_ref, wr_ref, wzt_ref,
                       wg_ref, bg_ref, q_ref, k_ref, v_ref, r_ref, gate_ref):
    hn = _rms(x_ref[...], g_ref[...]).astype(BF16)
    zt = lax.dot_general(wzt_ref[...], hn, NT_DIMS,
                         preferred_element_type=F32).astype(BF16)
    zg = lax.dot_general(zt, wg_ref[...], TN_DIMS,
                         preferred_element_type=F32) + bg_ref[...]
    gate_ref[...] = _log_sigmoid(zg) * (LOG2E / GLA_GATE_TAU)
    q = jnp.dot(hn, wq_ref[...], preferred_element_type=F32)
    q_ref[...] = q * GLA_SCALE
    k_ref[...] = jnp.dot(hn, wk_ref[...], preferred_element_type=F32)
    v_ref[...] = jnp.dot(hn, wv_ref[...], preferred_element_type=F32).astype(BF16)
    r_ref[...] = jnp.dot(hn, wr_ref[...], preferred_element_type=F32)


def _gla_inproj(h, g, wq, wk, wv, wr, wzt, wg, bg):
    tm = PROJ_TM
    rows = lambda i: (i, 0)
    return pl.pallas_call(
        _gla_inproj_kernel,
        grid=(TOKENS // tm,),
        in_specs=[
            pl.BlockSpec((tm, D_MODEL), rows),
            _resident((1, D_MODEL)),
            _resident((D_MODEL, GLA_KEY_DIM)),
            _resident((D_MODEL, GLA_KEY_DIM)),
            _resident((D_MODEL, D_MODEL)),
            _resident((D_MODEL, D_MODEL)),
            _resident((GLA_GATE_RANK, D_MODEL)),
            _resident((GLA_GATE_RANK, GLA_KEY_DIM)),
            _resident((1, GLA_KEY_DIM)),
        ],
        out_specs=[
            pl.BlockSpec((tm, GLA_KEY_DIM), rows),
            pl.BlockSpec((tm, GLA_KEY_DIM), rows),
            pl.BlockSpec((tm, D_MODEL), rows),
            pl.BlockSpec((tm, D_MODEL), rows),
            pl.BlockSpec((tm, GLA_KEY_DIM), rows),
        ],
        out_shape=[
            jax.ShapeDtypeStruct((TOKENS, GLA_KEY_DIM), F32),
            jax.ShapeDtypeStruct((TOKENS, GLA_KEY_DIM), F32),
            jax.ShapeDtypeStruct((TOKENS, D_MODEL), BF16),
            jax.ShapeDtypeStruct((TOKENS, D_MODEL), F32),
            jax.ShapeDtypeStruct((TOKENS, GLA_KEY_DIM), F32),
        ],
        compiler_params=_params("parallel"),
        name="gla_inproj",
    )(h, g, wq, wk, wv, wr, wzt, wg, bg)


GLA_FINE_LEVELS = (1, 2, 4)
GLA_COARSE_LEVELS = (8, 16, 32)
GLA_LEVELS = GLA_FINE_LEVELS + GLA_COARSE_LEVELS
GLA_STEP_CHUNKS = 8
SUBLANES = 8
assert max(GLA_FINE_LEVELS) < SUBLANES <= min(GLA_COARSE_LEVELS)


def _gla_selection_matrix():
    c = CHUNK
    t = np.arange(c)[:, None]
    r = np.arange(c)[None, :]
    mats = [(r > m * (t // m)) & (r <= t) for m in GLA_FINE_LEVELS[1:]]
    for m in GLA_FINE_LEVELS:
        nxt = np.minimum(m * (t // m + 1), c - 1)
        mats.append((r > t) & (r <= nxt))
    mats.append(r <= t)
    sel = np.concatenate(mats, axis=0).astype(np.float32)
    return np.concatenate([sel, sel], axis=1)


GLA_SEL_ROWS = 2 * len(GLA_FINE_LEVELS) * CHUNK


def _gla_chunk_kernel(q_ref, k_ref, v_ref, r_ref, gate_ref, sel_ref, gn_ref,
                      y_ref, state_ref):
    c, nh, dk, dv = CHUNK, GLA_HEADS, GLA_DK, GLA_DV

    @pl.when(pl.program_id(1) == 0)
    def _():
        state_ref[...] = jnp.zeros_like(state_ref)

    t_idx = lax.broadcasted_iota(jnp.int32, (nh * c, LANES), 0) % c
    l_idx = lax.broadcasted_iota(jnp.int32, (nh * c, LANES), 1)
    h_idx = lax.broadcasted_iota(jnp.int32, (nh * c, LANES), 0) // c
    s_idx = l_idx % c
    x = t_idx ^ s_idx
    level = sum((x >= m).astype(jnp.int32) for m in GLA_LEVELS)
    level = jnp.where((l_idx // c == h_idx % (LANES // c)) & (s_idx <= t_idx), level, -1)

    def stack_heads(a):
        return jnp.concatenate([a[:, h * dk:(h + 1) * dk] for h in range(nh)], axis=0)

    def band(qq, kk):
        full = lax.dot_general(stack_heads(qq.astype(BF16)), stack_heads(kk.astype(BF16)),
                               NT_DIMS, preferred_element_type=F32)
        per_tile = LANES // c
        return jnp.concatenate(
            [full[h * c:(h + 1) * c, (h // per_tile) * LANES:(h // per_tile + 1) * LANES]
             for h in range(nh)], axis=0)

    def blockwise(b, m, side):
        out = []
        for j in range(c // m):
            blk = b[j * m:(j + 1) * m, :]
            if side == "q":
                out.append(blk - b[j * m:j * m + 1, :])
            else:
                nxt = min((j + 1) * m, c - 1)
                out.append(b[nxt:nxt + 1, :] - blk)
        return jnp.concatenate(out, axis=0)

    n_fine = len(GLA_FINE_LEVELS)

    def state_free_part(rows):
        q = q_ref[rows, :]
        k = k_ref[rows, :]
        gate = gate_ref[rows, :]
        g_hi = gate.astype(BF16)
        g_lo = (gate - g_hi.astype(F32)).astype(BF16)
        sums = jnp.dot(sel_ref[...], jnp.concatenate([g_hi, g_lo], axis=0),
                       preferred_element_type=F32)
        fine = jnp.exp2(sums[:(2 * n_fine - 1) * c, :])
        b = sums[(2 * n_fine - 1) * c:, :]
        attn = jnp.where(level == 0, band(q, k), 0.0)
        for li, m in enumerate(GLA_LEVELS):
            if m in GLA_FINE_LEVELS:
                qq = q if li == 0 else q * fine[(li - 1) * c:li * c, :]
                kk = k * fine[(n_fine - 1 + li) * c:(n_fine + li) * c, :]
            else:
                qq = q * jnp.exp2(blockwise(b, m, "q"))
                kk = k * jnp.exp2(blockwise(b, m, "k"))
            attn = jnp.where(level == li + 1, band(qq, kk), attn)
        e_q = jnp.exp2(b)
        qs = (q * e_q).astype(BF16)
        kd = (k * jnp.exp2(b[c - 1:c, :] - b)).astype(BF16)
        return attn.astype(BF16), qs, kd, e_q[c - 1:c, :]

    chunks = [pl.ds(ci * c, c) for ci in range(GLA_STEP_CHUNKS)]
    parts = [state_free_part(rows) for rows in chunks]
    gn = gn_ref[...]
    for rows, (attn, qs, kd, e_last) in zip(chunks, parts):
        for h in range(nh):
            v_h = v_ref[rows, h * dv:(h + 1) * dv]
            state_t = state_ref[h]
            o = jnp.dot(attn[h * c:(h + 1) * c, :],
                        jnp.concatenate([v_h] * (LANES // c), axis=0),
                        preferred_element_type=F32)
            o = o + lax.dot_general(qs[:, h * dk:(h + 1) * dk], state_t.astype(BF16),
                                    NT_DIMS, preferred_element_type=F32)
            state_ref[h] = state_t * e_last[:, h * dk:(h + 1) * dk] + lax.dot_general(
                v_h, kd[:, h * dk:(h + 1) * dk], TN_DIMS, preferred_element_type=F32)
            o = _rms(o, gn)
            r = r_ref[rows, h * dv:(h + 1) * dv]
            y_ref[rows, h * dv:(h + 1) * dv] = (o * (r * jax.nn.sigmoid(r))).astype(BF16)


def _gla_chunks(q, k, v, r, gate, sel, gn):
    tm = GLA_STEP_CHUNKS * CHUNK
    n = SEQ // tm
    rows = lambda b, j: (b * n + j, 0)
    full = lambda b, j: (0, 0)
    return pl.pallas_call(
        _gla_chunk_kernel,
        grid=(BATCH, n),
        in_specs=[
            pl.BlockSpec((tm, GLA_KEY_DIM), rows),
            pl.BlockSpec((tm, GLA_KEY_DIM), rows),
            pl.BlockSpec((tm, D_MODEL), rows),
            pl.BlockSpec((tm, D_MODEL), rows),
            pl.BlockSpec((tm, GLA_KEY_DIM), rows),
            pl.BlockSpec((GLA_SEL_ROWS, 2 * CHUNK), full),
            pl.BlockSpec((1, GLA_DV), full),
        ],
        out_specs=pl.BlockSpec((tm, D_MODEL), rows),
        out_shape=jax.ShapeDtypeStruct((TOKENS, D_MODEL), BF16),
        scratch_shapes=[pltpu.VMEM((GLA_HEADS, GLA_DV, GLA_DK), F32)],
        compiler_params=_params("parallel", "arbitrary"),
        name="gla_chunks",
    )(q, k, v, r, gate, sel, gn)


def _mlp_kernel(a_ref, wo_ref, res_ref, g_ref, wup_ref, wdn_ref, gf_ref, o_ref,
                *, final_norm):
    x = res_ref[...] + jnp.dot(a_ref[...], wo_ref[...], preferred_element_type=F32)
    hn = _rms(x, g_ref[...])
    for f in range(D_FF // MLP_TF):
        cols = slice(f * MLP_TF, (f + 1) * MLP_TF)
        u = jnp.maximum(jnp.dot(hn, wup_ref[:, cols], preferred_element_type=F32), 0.0)
        x = x + jnp.dot(u * u, wdn_ref[cols, :], preferred_element_type=F32)
    o_ref[...] = _rms(x, gf_ref[...]) if final_norm else x


def _mlp(a, w_out, jo, res, g, w_up, w_down, layer, gf, final_norm):
    tm = MLP_TM
    rows = lambda i: (i, 0)

    def layer_weights(shape, idx):
        return pl.BlockSpec((None,) + shape, lambda i: (idx, 0, 0),
                            pipeline_mode=pl.Buffered(1))

    return pl.pallas_call(
        functools.partial(_mlp_kernel, final_norm=final_norm),
        grid=(TOKENS // tm,),
        in_specs=[
            pl.BlockSpec((tm, D_MODEL), rows),
            layer_weights((D_MODEL, D_MODEL), jo),
            pl.BlockSpec((tm, D_MODEL), rows),
            _resident((1, D_MODEL)),
            layer_weights((D_MODEL, D_FF), layer),
            layer_weights((D_FF, D_MODEL), layer),
            _resident((1, D_MODEL)),
        ],
        out_specs=pl.BlockSpec((tm, D_MODEL), rows),
        out_shape=jax.ShapeDtypeStruct((TOKENS, D_MODEL), F32),
        compiler_params=_params("parallel"),
        name="mlp",
    )(a, w_out, res, g, w_up, w_down, gf)


def _row(v):
    return v.reshape(1, -1).astype(F32)


def _fox_mixer(h, g, w_in, b_f):
    d = D_MODEL
    wqt = w_in[:, :d].T.astype(BF16)
    wk = w_in[:, d:2 * d].astype(BF16)
    wvt = w_in[:, 2 * d:3 * d].T.astype(BF16)
    wf = jnp.repeat(w_in[:, 3 * d:], FOX_GATE_LANES, axis=1).astype(BF16)
    bf = _row(jnp.repeat(b_f, FOX_GATE_LANES))
    qt, k, vt, ck = _fox_inproj(h, _row(g), wqt, wk, wvt, wf, bf)
    return _fox_attn(qt, k, ck, vt)


def _gla_mixer(h, g, w_in, w_gate_up, b_gate, g_norm, sel):
    kd, d = GLA_KEY_DIM, D_MODEL
    wq = w_in[:, :kd].astype(BF16)
    wk = w_in[:, kd:2 * kd].astype(BF16)
    wv = w_in[:, 2 * kd:2 * kd + d].astype(BF16)
    z0 = 2 * kd + d
    wzt = w_in[:, z0:z0 + GLA_GATE_RANK].T.astype(BF16)
    wr = w_in[:, z0 + GLA_GATE_RANK:].astype(BF16)
    wg = w_gate_up.astype(BF16)
    q, k, v, r, gate = _gla_inproj(h, _row(g), wq, wk, wv, wr, wzt, wg, _row(b_gate))
    return _gla_chunks(q, k, v, r, gate, sel, _row(g_norm))


def kernel(x, fox_w_in, fox_b_f, fox_w_out, gla_w_in, gla_w_gate_up, gla_b_gate,
           gla_norm_g, gla_w_out, mlp_w_up, mlp_w_down, norm_mix_g, norm_mlp_g,
           norm_final_g):
    assert x.shape == (BATCH, SEQ, D_MODEL) and x.dtype == F32
    sel = jnp.asarray(_gla_selection_matrix(), BF16)
    w_up, w_down = mlp_w_up, mlp_w_down
    w_out = (fox_w_out.astype(BF16), gla_w_out.astype(BF16))
    h = x.reshape(TOKENS, D_MODEL)
    for i in range(DEPTH):
        j = i // 2
        if i % 2 == 0:
            a = _fox_mixer(h, norm_mix_g[i], fox_w_in[j], fox_b_f[j])
        else:
            a = _gla_mixer(h, norm_mix_g[i], gla_w_in[j], gla_w_gate_up[j],
                           gla_b_gate[j], gla_norm_g[j], sel)
        h = _mlp(a, w_out[i % 2], j, h, _row(norm_mlp_g[i]), w_up, w_down, i,
                 _row(norm_final_g), final_norm=(i == DEPTH - 1))
    return h.reshape(BATCH, SEQ, D_MODEL)
```

```python
import functools

import numpy as np
import jax
import jax.numpy as jnp
from jax import lax
from jax.experimental import pallas as pl
from jax.experimental.pallas import tpu as pltpu

F32 = jnp.float32
BF16 = jnp.bfloat16

D_MODEL = 1024
BATCH = 4
SEQ = 4096
TOKENS = BATCH * SEQ
DEPTH = 4
EPS = 1e-6

FOX_HEADS = 16
FOX_HEAD_DIM = D_MODEL // FOX_HEADS
FOX_SCALE = FOX_HEAD_DIM ** -0.5
FOX_GATE_LANES = 8
FOX_BIAS_TERMS = 3
GLA_HEADS = 4
GLA_KEY_DIM = D_MODEL // 2
GLA_DK = GLA_KEY_DIM // GLA_HEADS
GLA_DV = D_MODEL // GLA_HEADS
GLA_GATE_RANK = 16
GLA_GATE_TAU = 16.0
GLA_SCALE = GLA_DK ** -0.5
CHUNK = 64
D_FF = 4 * D_MODEL

LANES = 128
VMEM_LIMIT = 56 * 1024 * 1024

PROJ_TM = 1024
MLP_TM = 512
MLP_TF = 1024
ATT_TQ = 2048
ATT_TK = 512
ATT_KV_UNROLL = 4
ATT_SCORE_LEAD = 2
ATT_PV_LAG = 1
assert ATT_TQ % (ATT_TK * ATT_KV_UNROLL) == 0
ATT_STRIP = 512
ATT_ONES_ROWS = 16
LOG2E = 1.4426950408889634
GATE_BLOCK = 256

NT_DIMS = (((1,), (1,)), ((), ()))
TN_DIMS = (((0,), (0,)), ((), ()))


def _params(*semantics):
    return pltpu.CompilerParams(dimension_semantics=semantics,
                                vmem_limit_bytes=VMEM_LIMIT)


def _resident(shape):
    return pl.BlockSpec(shape, lambda *_: (0,) * len(shape),
                        pipeline_mode=pl.Buffered(1))


def _rms(x, g):
    ms = jnp.mean(x * x, axis=-1, keepdims=True)
    return x * lax.rsqrt(ms + EPS) * g


def _log_sigmoid(x):
    return jnp.minimum(x, 0.0) - jnp.log1p(jnp.exp(-jnp.abs(x)))


def _split3(x):
    hi = x.astype(BF16)
    r1 = x - hi.astype(F32)
    mid = r1.astype(BF16)
    lo = (r1 - mid.astype(F32)).astype(BF16)
    return hi, mid, lo


def _fox_inproj_kernel(x_ref, g_ref, wqt_ref, wk_ref, wvt_ref, wf_ref, bf_ref,
                       qt_ref, k_ref, vt_ref, ck_ref, carry_ref):
    bl = GATE_BLOCK

    @pl.when(pl.program_id(0) % (SEQ // PROJ_TM) == 0)
    def _():
        carry_ref[...] = jnp.zeros_like(carry_ref)

    hn = _rms(x_ref[...], g_ref[...]).astype(BF16)

    qt = lax.dot_general(wqt_ref[...], hn, NT_DIMS, preferred_element_type=F32)
    qt_ref[0] = (qt * (FOX_SCALE * LOG2E)).astype(BF16)

    lf = _log_sigmoid(jnp.dot(hn, wf_ref[...], preferred_element_type=F32) + bf_ref[...])
    row = lax.broadcasted_iota(jnp.int32, (bl, bl), 0)
    col = lax.broadcasted_iota(jnp.int32, (bl, bl), 1)
    tril = (col <= row).astype(BF16)
    j = lax.broadcasted_iota(jnp.int32, (1, LANES), 1) % FOX_GATE_LANES
    carry = carry_ref[...]
    for i in range(PROJ_TM // bl):
        hi, mid, lo = _split3(lf[i * bl:(i + 1) * bl, :])
        c = carry + (jnp.dot(tril, hi, preferred_element_type=F32)
                     + jnp.dot(tril, mid, preferred_element_type=F32)
                     + jnp.dot(tril, lo, preferred_element_type=F32))
        chi, cmid, clo = (t.astype(F32) for t in _split3(c * LOG2E))
        ck = jnp.where(j == 0, -chi, jnp.where(j == 1, -cmid, jnp.where(
            j == 2, -clo, 0.0)))
        ck_ref[i * bl:(i + 1) * bl, :] = ck.astype(BF16)
        carry = c[bl - 1:bl, :]
    carry_ref[...] = carry

    k_ref[...] = jnp.dot(hn, wk_ref[...], preferred_element_type=F32).astype(BF16)
    vt = lax.dot_general(wvt_ref[...], hn, NT_DIMS, preferred_element_type=F32)
    vt_ref[0] = vt.astype(BF16)


def _fox_inproj(h, g, wqt, wk, wvt, wf, bf):
    tm = PROJ_TM
    per_b = SEQ // tm
    return pl.pallas_call(
        _fox_inproj_kernel,
        grid=(TOKENS // tm,),
        in_specs=[
            pl.BlockSpec((tm, D_MODEL), lambda i: (i, 0)),
            _resident((1, D_MODEL)),
            _resident((D_MODEL, D_MODEL)),
            _resident((D_MODEL, D_MODEL)),
            _resident((D_MODEL, D_MODEL)),
            _resident((D_MODEL, LANES)),
            _resident((1, LANES)),
        ],
        out_specs=[
            pl.BlockSpec((1, D_MODEL, tm), lambda i: (i // per_b, 0, i % per_b)),
            pl.BlockSpec((tm, D_MODEL), lambda i: (i, 0)),
            pl.BlockSpec((1, D_MODEL, tm), lambda i: (i // per_b, 0, i % per_b)),
            pl.BlockSpec((tm, LANES), lambda i: (i, 0)),
        ],
        out_shape=[
            jax.ShapeDtypeStruct((BATCH, D_MODEL, SEQ), BF16),
            jax.ShapeDtypeStruct((TOKENS, D_MODEL), BF16),
            jax.ShapeDtypeStruct((BATCH, D_MODEL, SEQ), BF16),
            jax.ShapeDtypeStruct((TOKENS, LANES), BF16),
        ],
        scratch_shapes=[pltpu.VMEM((1, LANES), F32)],
        compiler_params=_params("arbitrary"),
        name="fox_inproj",
    )(h, g, wqt, wk, wvt, wf, bf)


def _fox_attn_kernel(qt_ref, k_ref, ck_ref, perm_ref, vt_ref, o_ref, kcat_ref, qcat_ref):
    tq, tk, hd, w = ATT_TQ, ATT_TK, FOX_HEAD_DIM, ATT_STRIP

    nb = FOX_BIAS_TERMS
    bias = jnp.dot(ck_ref[...], perm_ref[...], preferred_element_type=F32)
    keys = k_ref[...].astype(F32)
    lane = lax.broadcasted_iota(jnp.int32, (SEQ, LANES), 1)
    key_side = jnp.where(lane % hd < nb, bias, jnp.where(lane % hd < 2 * nb, 1.0, 0.0))
    queries = qt_ref[0].astype(F32)
    row = lax.broadcasted_iota(jnp.int32, (LANES, SEQ), 0)
    query_side = jnp.where(row % hd < nb, 1.0,
                           jnp.where(row % hd < 2 * nb, -bias.T, 0.0))
    for hh in range(2):
        kcat_ref[hh] = jnp.where(lane // hd == hh, keys, key_side).astype(BF16)
        qcat_ref[hh] = jnp.where(row // hd == hh, queries, query_side).astype(BF16)

    strips, tiles_per_q = tq // w, tq // tk
    units = [(qi, hh, st) for qi in range(SEQ // tq) for st in range(strips)
             for hh in range(2)]
    of_tile = lambda qi: [u for u in range(len(units)) if units[u][0] == qi]
    ones = jnp.ones((ATT_ONES_ROWS, tk), BF16)

    def scores(kj, diag, u):
        qi, hh, st = units[u]
        k0 = pl.multiple_of(kj * tk, tk)
        s = jnp.dot(kcat_ref[hh, pl.ds(k0, tk), :],
                    qcat_ref[hh, :, pl.ds(qi * tq + st * w, w)],
                    preferred_element_type=F32)
        if diag is not None and (diag + 1) * tk > st * w + 1:
            s_idx = lax.broadcasted_iota(jnp.int32, (tk, w), 0) + diag * tk
            t_idx = lax.broadcasted_iota(jnp.int32, (tk, w), 1) + st * w
            s = jnp.where(s_idx <= t_idx, s, -jnp.inf)
        return s

    def softmax(m, s):
        m_new = jnp.maximum(m, jnp.max(s, axis=0, keepdims=True))
        return m_new, jnp.exp2(m - m_new), jnp.exp2(s - m_new).astype(BF16)

    def weighted(kj, u, alpha, p, acc):
        hh = units[u][1]
        k0 = pl.multiple_of(kj * tk, tk)
        vt = jnp.concatenate(
            [vt_ref[0, pl.ds(hh * hd, hd), pl.ds(k0, tk)], ones], axis=0)
        return alpha * acc + jnp.dot(vt, p, preferred_element_type=F32)

    def run(work, state):
        la, lb = ATT_SCORE_LEAD, ATT_SCORE_LEAD + ATT_PV_LAG
        state, s_of, sm_of, n = dict(state), {}, {}, len(work)
        assert all(work[i][2] != work[j][2]
                   for i in range(n) for j in range(max(0, i - (lb - la)), i))
        for step in range(n + lb):
            if step < n:
                s_of[step] = scores(*work[step])
            if la <= step < n + la:
                u = work[step - la][2]
                sm_of[step - la] = softmax(state[u][0], s_of.pop(step - la))
            if step >= lb:
                kj, _, u = work[step - lb]
                m_new, alpha, p = sm_of.pop(step - lb)
                state[u] = (m_new, weighted(kj, u, alpha, p, state[u][1]))
        return state

    state = {u: (jnp.full((1, w), -jnp.inf, F32),
                 jnp.zeros((hd + ATT_ONES_ROWS, w), F32)) for u in range(len(units))}

    for qi in range(1, SEQ // tq):
        mine = of_tile(qi)

        def full_tiles(i, carry, mine=mine):
            out = run([(i * ATT_KV_UNROLL + j, None, u)
                       for j in range(ATT_KV_UNROLL) for u in mine], dict(zip(mine, carry)))
            return tuple(out[u] for u in mine)

        trips = qi * tiles_per_q // ATT_KV_UNROLL
        carry = tuple(state[u] for u in mine)
        carry = (full_tiles(0, carry) if trips == 1
                 else lax.fori_loop(0, trips, full_tiles, carry))
        state.update(zip(mine, carry))

    per_tile = [[(qi * tiles_per_q + d, d, u) for d in range(tiles_per_q)
                 for u in of_tile(qi) if (units[u][2] + 1) * w > d * tk]
                for qi in range(SEQ // tq)]
    state = run([item for group in zip(*per_tile) for item in group], state)

    for qi in range(SEQ // tq):
        heads = []
        for hh in range(2):
            acc = jnp.concatenate([state[u][1] for u in of_tile(qi)
                                   if units[u][1] == hh], axis=1)
            heads.append(acc[:hd] / acc[hd:hd + 1])
        o = jnp.concatenate(heads, axis=0)
        o_ref[qi * tq:(qi + 1) * tq, :] = o.T.astype(BF16)


def _fox_bias_permutation():
    pairs, hd = FOX_HEADS // 2, FOX_HEAD_DIM
    perm = np.zeros((pairs, LANES, LANES), np.float32)
    for p in range(pairs):
        for hh in range(2):
            for j in range(FOX_BIAS_TERMS):
                src = FOX_GATE_LANES * (2 * p + hh) + j
                perm[p, src, (1 - hh) * hd + j] = 1.0
                perm[p, src, (1 - hh) * hd + FOX_BIAS_TERMS + j] = 1.0
    return perm


def _fox_attn(qt, k, ck, vt):
    pairs = FOX_HEADS // 2
    perm = jnp.asarray(_fox_bias_permutation(), BF16)
    return pl.pallas_call(
        _fox_attn_kernel,
        grid=(BATCH, pairs),
        in_specs=[
            pl.BlockSpec((1, LANES, SEQ), lambda b, p: (b, p, 0)),
            pl.BlockSpec((SEQ, LANES), lambda b, p: (b, p)),
            pl.BlockSpec((SEQ, LANES), lambda b, p: (b, 0)),
            pl.BlockSpec((None, LANES, LANES), lambda b, p: (p, 0, 0)),
            pl.BlockSpec((1, LANES, SEQ), lambda b, p: (b, p, 0)),
        ],
        out_specs=pl.BlockSpec((SEQ, LANES), lambda b, p: (b, p)),
        out_shape=jax.ShapeDtypeStruct((TOKENS, D_MODEL), BF16),
        scratch_shapes=[pltpu.VMEM((2, SEQ, LANES), BF16),
                        pltpu.VMEM((2, LANES, SEQ), BF16)],
        compiler_params=_params("parallel", "parallel"),
        name="fox_attn",
    )(qt, k, ck, perm, vt)


def _gla_inproj_kernel(x_ref, g_ref, wq_ref, wk_ref, wv_ref, wr_ref, wzt_ref,
                       wg_ref, bg_ref, q_ref, k_ref, v_ref, r_ref, gate_ref):
    hn = _rms(x_ref[...], g_ref[...]).astype(BF16)
    zt = lax.dot_general(wzt_ref[...], hn, NT_DIMS,
                         preferred_element_type=F32).astype(BF16)
    zg = lax.dot_general(zt, wg_ref[...], TN_DIMS,
                         preferred_element_type=F32) + bg_ref[...]
    gate_ref[...] = _log_sigmoid(zg) * (LOG2E / GLA_GATE_TAU)
    q = jnp.dot(hn, wq_ref[...], preferred_element_type=F32)
    q_ref[...] = q * GLA_SCALE
    k_ref[...] = jnp.dot(hn, wk_ref[...], preferred_element_type=F32)
    v_ref[...] = jnp.dot(hn, wv_ref[...], preferred_element_type=F32).astype(BF16)
    r_ref[...] = jnp.dot(hn, wr_ref[...], preferred_element_type=F32)


def _gla_inproj(h, g, wq, wk, wv, wr, wzt, wg, bg):
    tm = PROJ_TM
    rows = lambda i: (i, 0)
    return pl.pallas_call(
        _gla_inproj_kernel,
        grid=(TOKENS // tm,),
        in_specs=[
            pl.BlockSpec((tm, D_MODEL), rows),
            _resident((1, D_MODEL)),
            _resident((D_MODEL, GLA_KEY_DIM)),
            _resident((D_MODEL, GLA_KEY_DIM)),
            _resident((D_MODEL, D_MODEL)),
            _resident((D_MODEL, D_MODEL)),
            _resident((GLA_GATE_RANK, D_MODEL)),
            _resident((GLA_GATE_RANK, GLA_KEY_DIM)),
            _resident((1, GLA_KEY_DIM)),
        ],
        out_specs=[
            pl.BlockSpec((tm, GLA_KEY_DIM), rows),
            pl.BlockSpec((tm, GLA_KEY_DIM), rows),
            pl.BlockSpec((tm, D_MODEL), rows),
            pl.BlockSpec((tm, D_MODEL), rows),
            pl.BlockSpec((tm, GLA_KEY_DIM), rows),
        ],
        out_shape=[
            jax.ShapeDtypeStruct((TOKENS, GLA_KEY_DIM), F32),
            jax.ShapeDtypeStruct((TOKENS, GLA_KEY_DIM), F32),
            jax.ShapeDtypeStruct((TOKENS, D_MODEL), BF16),
            jax.ShapeDtypeStruct((TOKENS, D_MODEL), F32),
            jax.ShapeDtypeStruct((TOKENS, GLA_KEY_DIM), F32),
        ],
        compiler_params=_params("parallel"),
        name="gla_inproj",
    )(h, g, wq, wk, wv, wr, wzt, wg, bg)


GLA_FINE_LEVELS = (1, 2, 4)
GLA_COARSE_LEVELS = (8, 16, 32)
GLA_LEVELS = GLA_FINE_LEVELS + GLA_COARSE_LEVELS
GLA_STEP_CHUNKS = 16
SUBLANES = 8
assert max(GLA_FINE_LEVELS) < SUBLANES <= min(GLA_COARSE_LEVELS)


def _gla_selection_matrix():
    c = CHUNK
    t = np.arange(c)[:, None]
    r = np.arange(c)[None, :]
    mats = [(r > m * (t // m)) & (r <= t) for m in GLA_FINE_LEVELS[1:]]
    for m in GLA_FINE_LEVELS:
        nxt = np.minimum(m * (t // m + 1), c - 1)
        mats.append((r > t) & (r <= nxt))
    mats.append(r <= t)
    sel = np.concatenate(mats, axis=0).astype(np.float32)
    return np.concatenate([sel, sel], axis=1)


GLA_SEL_ROWS = 2 * len(GLA_FINE_LEVELS) * CHUNK


def _gla_chunk_kernel(q_ref, k_ref, v_ref, r_ref, gate_ref, sel_ref, gn_ref,
                      y_ref, state_ref):
    c, nh, dk, dv = CHUNK, GLA_HEADS, GLA_DK, GLA_DV

    @pl.when(pl.program_id(1) == 0)
    def _():
        state_ref[...] = jnp.zeros_like(state_ref)

    t_idx = lax.broadcasted_iota(jnp.int32, (nh * c, LANES), 0) % c
    l_idx = lax.broadcasted_iota(jnp.int32, (nh * c, LANES), 1)
    h_idx = lax.broadcasted_iota(jnp.int32, (nh * c, LANES), 0) // c
    s_idx = l_idx % c
    x = t_idx ^ s_idx
    level = sum((x >= m).astype(jnp.int32) for m in GLA_LEVELS)
    level = jnp.where((l_idx // c == h_idx % (LANES // c)) & (s_idx <= t_idx), level, -1)

    def stack_heads(a):
        return jnp.concatenate([a[:, h * dk:(h + 1) * dk] for h in range(nh)], axis=0)

    def band(qq, kk):
        full = lax.dot_general(stack_heads(qq.astype(BF16)), stack_heads(kk.astype(BF16)),
                               NT_DIMS, preferred_element_type=F32)
        per_tile = LANES // c
        return jnp.concatenate(
            [full[h * c:(h + 1) * c, (h // per_tile) * LANES:(h // per_tile + 1) * LANES]
             for h in range(nh)], axis=0)

    def blockwise(b, m, side):
        out = []
        for j in range(c // m):
            blk = b[j * m:(j + 1) * m, :]
            if side == "q":
                out.append(blk - b[j * m:j * m + 1, :])
            else:
                nxt = min((j + 1) * m, c - 1)
                out.append(b[nxt:nxt + 1, :] - blk)
        return jnp.concatenate(out, axis=0)

    n_fine = len(GLA_FINE_LEVELS)

    def state_free_part(rows):
        q = q_ref[rows, :]
        k = k_ref[rows, :]
        gate = gate_ref[rows, :]
        g_hi = gate.astype(BF16)
        g_lo = (gate - g_hi.astype(F32)).astype(BF16)
        sums = jnp.dot(sel_ref[...], jnp.concatenate([g_hi, g_lo], axis=0),
                       preferred_element_type=F32)
        fine = jnp.exp2(sums[:(2 * n_fine - 1) * c, :])
        b = sums[(2 * n_fine - 1) * c:, :]
        attn = jnp.where(level == 0, band(q, k), 0.0)
        for li, m in enumerate(GLA_LEVELS):
            if m in GLA_FINE_LEVELS:
                qq = q if li == 0 else q * fine[(li - 1) * c:li * c, :]
                kk = k * fine[(n_fine - 1 + li) * c:(n_fine + li) * c, :]
            else:
                qq = q * jnp.exp2(blockwise(b, m, "q"))
                kk = k * jnp.exp2(blockwise(b, m, "k"))
            attn = jnp.where(level == li + 1, band(qq, kk), attn)
        e_q = jnp.exp2(b)
        qs = (q * e_q).astype(BF16)
        kd = (k * jnp.exp2(b[c - 1:c, :] - b)).astype(BF16)
        return attn.astype(BF16), qs, kd, e_q[c - 1:c, :]

    chunks = [pl.ds(ci * c, c) for ci in range(GLA_STEP_CHUNKS)]
    parts = [state_free_part(rows) for rows in chunks]
    gn = gn_ref[...]
    for rows, (attn, qs, kd, e_last) in zip(chunks, parts):
        for h in range(nh):
            v_h = v_ref[rows, h * dv:(h + 1) * dv]
            state_t = state_ref[h]
            o = jnp.dot(attn[h * c:(h + 1) * c, :],
                        jnp.concatenate([v_h] * (LANES // c), axis=0),
                        preferred_element_type=F32)
            o = o + lax.dot_general(qs[:, h * dk:(h + 1) * dk], state_t.astype(BF16),
                                    NT_DIMS, preferred_element_type=F32)
            state_ref[h] = state_t * e_last[:, h * dk:(h + 1) * dk] + lax.dot_general(
                v_h, kd[:, h * dk:(h + 1) * dk], TN_DIMS, preferred_element_type=F32)
            o = _rms(o, gn)
            r = r_ref[rows, h * dv:(h + 1) * dv]
            y_ref[rows, h * dv:(h + 1) * dv] = (o * (r * jax.nn.sigmoid(r))).astype(BF16)


def _gla_chunks(q, k, v, r, gate, sel, gn):
    tm = GLA_STEP_CHUNKS * CHUNK
    n = SEQ // tm
    rows = lambda b, j: (b * n + j, 0)
    full = lambda b, j: (0, 0)
    return pl.pallas_call(
        _gla_chunk_kernel,
        grid=(BATCH, n),
        in_specs=[
            pl.BlockSpec((tm, GLA_KEY_DIM), rows),
            pl.BlockSpec((tm, GLA_KEY_DIM), rows),
            pl.BlockSpec((tm, D_MODEL), rows),
            pl.BlockSpec((tm, D_MODEL), rows),
            pl.BlockSpec((tm, GLA_KEY_DIM), rows),
            pl.BlockSpec((GLA_SEL_ROWS, 2 * CHUNK), full),
            pl.BlockSpec((1, GLA_DV), full),
        ],
        out_specs=pl.BlockSpec((tm, D_MODEL), rows),
        out_shape=jax.ShapeDtypeStruct((TOKENS, D_MODEL), BF16),
        scratch_shapes=[pltpu.VMEM((GLA_HEADS, GLA_DV, GLA_DK), F32)],
        compiler_params=_params("parallel", "arbitrary"),
        name="gla_chunks",
    )(q, k, v, r, gate, sel, gn)


def _mlp_kernel(a_ref, wo_ref, res_ref, g_ref, wup_ref, wdn_ref, gf_ref, o_ref,
                *, final_norm):
    x = res_ref[...] + jnp.dot(a_ref[...], wo_ref[...], preferred_element_type=F32)
    hn = _rms(x, g_ref[...])
    for f in range(D_FF // MLP_TF):
        cols = slice(f * MLP_TF, (f + 1) * MLP_TF)
        u = jnp.maximum(jnp.dot(hn, wup_ref[:, cols], preferred_element_type=F32), 0.0)
        x = x + jnp.dot(u * u, wdn_ref[cols, :], preferred_element_type=F32)
    o_ref[...] = _rms(x, gf_ref[...]) if final_norm else x


def _mlp(a, w_out, jo, res, g, w_up, w_down, layer, gf, final_norm):
    tm = MLP_TM
    rows = lambda i: (i, 0)

    def layer_weights(shape, idx):
        return pl.BlockSpec((None,) + shape, lambda i: (idx, 0, 0),
                            pipeline_mode=pl.Buffered(1))

    return pl.pallas_call(
        functools.partial(_mlp_kernel, final_norm=final_norm),
        grid=(TOKENS // tm,),
        in_specs=[
            pl.BlockSpec((tm, D_MODEL), rows),
            layer_weights((D_MODEL, D_MODEL), jo),
            pl.BlockSpec((tm, D_MODEL), rows),
            _resident((1, D_MODEL)),
            layer_weights((D_MODEL, D_FF), layer),
            layer_weights((D_FF, D_MODEL), layer),
            _resident((1, D_MODEL)),
        ],
        out_specs=pl.BlockSpec((tm, D_MODEL), rows),
        out_shape=jax.ShapeDtypeStruct((TOKENS, D_MODEL), F32),
        compiler_params=_params("parallel"),
        name="mlp",
    )(a, w_out, res, g, w_up, w_down, gf)


def _row(v):
    return v.reshape(1, -1).astype(F32)


def _fox_mixer(h, g, w_in, b_f):
    d = D_MODEL
    wqt = w_in[:, :d].T.astype(BF16)
    wk = w_in[:, d:2 * d].astype(BF16)
    wvt = w_in[:, 2 * d:3 * d].T.astype(BF16)
    wf = jnp.repeat(w_in[:, 3 * d:], FOX_GATE_LANES, axis=1).astype(BF16)
    bf = _row(jnp.repeat(b_f, FOX_GATE_LANES))
    qt, k, vt, ck = _fox_inproj(h, _row(g), wqt, wk, wvt, wf, bf)
    return _fox_attn(qt, k, ck, vt)


def _gla_mixer(h, g, w_in, w_gate_up, b_gate, g_norm, sel):
    kd, d = GLA_KEY_DIM, D_MODEL
    wq = w_in[:, :kd].astype(BF16)
    wk = w_in[:, kd:2 * kd].astype(BF16)
    wv = w_in[:, 2 * kd:2 * kd + d].astype(BF16)
    z0 = 2 * kd + d
    wzt = w_in[:, z0:z0 + GLA_GATE_RANK].T.astype(BF16)
    wr = w_in[:, z0 + GLA_GATE_RANK:].astype(BF16)
    wg = w_gate_up.astype(BF16)
    q, k, v, r, gate = _gla_inproj(h, _row(g), wq, wk, wv, wr, wzt, wg, _row(b_gate))
    return _gla_chunks(q, k, v, r, gate, sel, _row(g_norm))


def kernel(x, fox_w_in, fox_b_f, fox_w_out, gla_w_in, gla_w_gate_up, gla_b_gate,
           gla_norm_g, gla_w_out, mlp_w_up, mlp_w_down, norm_mix_g, norm_mlp_g,
           norm_final_g):
    assert x.shape == (BATCH, SEQ, D_MODEL) and x.dtype == F32
    sel = jnp.asarray(_gla_selection_matrix(), BF16)
    w_up, w_down = mlp_w_up, mlp_w_down
    w_out = (fox_w_out.astype(BF16), gla_w_out.astype(BF16))
    h = x.reshape(TOKENS, D_MODEL)
    for i in range(DEPTH):
        j = i // 2
        if i % 2 == 0:
            a = _fox_mixer(h, norm_mix_g[i], fox_w_in[j], fox_b_f[j])
        else:
            a = _gla_mixer(h, norm_mix_g[i], gla_w_in[j], gla_w_gate_up[j],
                           gla_b_gate[j], gla_norm_g[j], sel)
        h = _mlp(a, w_out[i % 2], j, h, _row(norm_mlp_g[i]), w_up, w_down, i,
                 _row(norm_final_g), final_norm=(i == DEPTH - 1))
    return h.reshape(BATCH, SEQ, D_MODEL)
```
